```python
import math
import jax, jax.numpy as jnp
from jax import lax
import numpy as np

D_MODEL = 4096
BATCH = 2
SEQ = 8192
DEPTH = 2

N_A = DEPTH // 2
N_B = DEPTH - N_A
MIX_WIDTH = D_MODEL
MEM_WIDTH = MIX_WIDTH // 4
MAIN_WIDTH = MIX_WIDTH - MEM_WIDTH
MEM_TOKENS = 256
MEM_HEADS = 4
MEM_HEAD_DIM = MEM_WIDTH // MEM_HEADS
CONV_CH = MAIN_WIDTH
CONV_WIDTH = 31
DIFF_HEAD_DIM = 128
DIFF_HEADS = MAIN_WIDTH // (2 * DIFF_HEAD_DIM)
DIFF_QK = DIFF_HEADS * 2 * DIFF_HEAD_DIM
DIFF_V = DIFF_HEADS * 2 * DIFF_HEAD_DIM
Q_BLOCK = 128
ROPE_THETA = 500000.0
ROT_DIM = DIFF_HEAD_DIM // 4
D_FF = 4 * D_MODEL
EPS = 1e-6

kernel_name = "yoco_conformer_diffattn_memory_hybrid"


def rmsnorm(t, g):
    t32 = t.astype(jnp.float32)
    inv = lax.rsqrt(jnp.mean(t32 * t32, axis=-1, keepdims=True) + EPS)
    return (t32 * inv).astype(t.dtype) * g


def layernorm(t, g, b):
    t32 = t.astype(jnp.float32)
    mu = jnp.mean(t32, axis=-1, keepdims=True)
    var = jnp.mean(jnp.square(t32 - mu), axis=-1, keepdims=True)
    return ((t32 - mu) * lax.rsqrt(var + EPS)).astype(t.dtype) * g + b


def rope_tables(positions):
    inv_freq = jnp.power(jnp.float32(ROPE_THETA), -jnp.arange(0, ROT_DIM, 2, dtype=jnp.float32) / ROT_DIM)
    ang = positions.astype(jnp.float32)[..., None] * inv_freq
    return jnp.cos(ang), jnp.sin(ang)


def partial_rope(t, cos, sin):
    half = ROT_DIM // 2
    c = cos[:, :, None, None, :].astype(t.dtype)
    s = sin[:, :, None, None, :].astype(t.dtype)
    r1, r2, rest = t[..., :half], t[..., half:ROT_DIM], t[..., ROT_DIM:]
    return jnp.concatenate([r1 * c - r2 * s, r2 * c + r1 * s, rest], axis=-1)


def causal_depthwise_conv(u, w, b):
    y = lax.conv_general_dilated(
        u, w[:, None, :], window_strides=(1,), padding=[(CONV_WIDTH - 1, 0)],
        dimension_numbers=('NWC', 'WIO', 'NWC'), feature_group_count=u.shape[-1])
    return y + b


def memory_attention(q, mem_n, w_mem_kv):
    B, S = q.shape[0], q.shape[1]
    kv = (mem_n @ w_mem_kv).reshape(B, mem_n.shape[1], 2, MEM_HEADS, MEM_HEAD_DIM)
    mk, mv = kv[:, :, 0], kv[:, :, 1]
    s = jnp.einsum('bshd,bmhd->bhsm', q, mk).astype(jnp.float32) * (MEM_HEAD_DIM ** -0.5)
    p = jax.nn.softmax(s, axis=-1).astype(q.dtype)
    return jnp.einsum('bhsm,bmhd->bshd', p, mv).reshape(B, S, MEM_WIDTH)


def diff_attention(q, k, v, lam):
    B, S, H, _, Dh = q.shape
    nb = S // Q_BLOCK
    qb = q.reshape(B, nb, Q_BLOCK, H, 2, Dh).transpose(1, 0, 2, 3, 4, 5)
    k_idx = jnp.arange(S)
    scale = Dh ** -0.5

    def one_block(args):
        i, q_blk = args
        s = jnp.einsum('bqhcd,bkhcd->bhcqk', q_blk, k).astype(jnp.float32) * scale
        q_idx = i * Q_BLOCK + jnp.arange(Q_BLOCK)
        mask = k_idx[None, :] <= q_idx[:, None]
        p = jax.nn.softmax(jnp.where(mask, s, -jnp.inf), axis=-1)
        a = p[:, :, 0] - lam.astype(jnp.float32) * p[:, :, 1]
        return jnp.einsum('bhqk,bkhe->bqhe', a.astype(v.dtype), v)

    o = lax.map(one_block, (jnp.arange(nb), qb))
    return o.transpose(1, 0, 2, 3, 4).reshape(B, S, H, v.shape[-1])


def sq_relu_mlp(h, w_up, w_down):
    return jnp.square(jax.nn.relu(h @ w_up)) @ w_down


def setup_inputs(seed: int = 0) -> dict:
    key = jax.random.key(seed)
    ks = iter(jax.random.split(key, 40))
    f32 = jnp.float32

    def nrm(shape, fan_in):
        return jax.random.normal(next(ks), shape, f32) * (fan_in ** -0.5)

    def gain(shape):
        return 1.0 + 0.02 * jax.random.normal(next(ks), shape, f32)

    def small(shape, scale=0.02):
        return scale * jax.random.normal(next(ks), shape, f32)

    x = jax.random.normal(next(ks), (BATCH, SEQ, D_MODEL), f32)
    mem = jax.random.normal(next(ks), (BATCH, MEM_TOKENS, D_MODEL), f32)
    offs = jax.random.randint(next(ks), (BATCH, 1), 0, 1024, dtype=jnp.int32)
    positions = jnp.arange(SEQ, dtype=jnp.int32)[None, :] + offs
    return {
        "x": x,
        "mem": mem,
        "positions": positions,
        "a_norm_pre": gain((N_A, D_MODEL)),
        "a_norm_post": gain((N_A, D_MODEL)),
        "a_mem_norm": gain((N_A, D_MODEL)),
        "a_w_in": nrm((N_A, D_MODEL, 2 * CONV_CH + MEM_WIDTH), D_MODEL),
        "a_conv_w": nrm((N_A, CONV_WIDTH, CONV_CH), CONV_WIDTH),
        "a_conv_b": small((N_A, CONV_CH)),
        "a_ln_g": gain((N_A, CONV_CH)),
        "a_ln_b": small((N_A, CONV_CH)),
        "a_w_mem_kv": nrm((N_A, D_MODEL, 2 * MEM_WIDTH), D_MODEL),
        "a_w_out": nrm((N_A, MIX_WIDTH, D_MODEL), MIX_WIDTH),
        "kv_norm": gain((D_MODEL,)),
        "w_kv": nrm((D_MODEL, DIFF_QK + DIFF_V), D_MODEL),
        "b_norm_pre": gain((N_B, D_MODEL)),
        "b_norm_post": gain((N_B, D_MODEL)),
        "b_mem_norm": gain((N_B, D_MODEL)),
        "b_w_in": nrm((N_B, D_MODEL, DIFF_QK + MEM_WIDTH), D_MODEL),
        "b_lambda": small((N_B, 4, DIFF_HEAD_DIM), 0.1),
        "b_subln": gain((N_B, 2 * DIFF_HEAD_DIM)),
        "b_w_mem_kv": nrm((N_B, D_MODEL, 2 * MEM_WIDTH), D_MODEL),
        "b_w_out": nrm((N_B, MIX_WIDTH, D_MODEL), MIX_WIDTH),
        "mlp_norm_pre": gain((DEPTH, D_MODEL)),
        "mlp_norm_post": gain((DEPTH, D_MODEL)),
        "w_up": nrm((DEPTH, D_MODEL, D_FF), D_MODEL),
        "w_down": nrm((DEPTH, D_FF, D_MODEL), D_FF),
    }


def reference(x, mem, positions,
              a_norm_pre, a_norm_post, a_mem_norm, a_w_in, a_conv_w, a_conv_b,
              a_ln_g, a_ln_b, a_w_mem_kv, a_w_out,
              kv_norm, w_kv,
              b_norm_pre, b_norm_post, b_mem_norm, b_w_in, b_lambda, b_subln,
              b_w_mem_kv, b_w_out,
              mlp_norm_pre, mlp_norm_post, w_up, w_down):
    B, S, _ = x.shape
    cos, sin = rope_tables(positions)
    k_sh = None
    v_sh = None
    for layer in range(DEPTH):
        if layer < N_A:
            i = layer
            h = rmsnorm(x, a_norm_pre[i])
            z = h @ a_w_in[i]
            u, g, qm = z[..., :CONV_CH], z[..., CONV_CH:2 * CONV_CH], z[..., 2 * CONV_CH:]
            c = u * jax.nn.sigmoid(g)
            c = causal_depthwise_conv(c, a_conv_w[i], a_conv_b[i])
            c = jax.nn.silu(layernorm(c, a_ln_g[i], a_ln_b[i]))
            m = memory_attention(qm.reshape(B, S, MEM_HEADS, MEM_HEAD_DIM),
                                 rmsnorm(mem, a_mem_norm[i]), a_w_mem_kv[i])
            y = jnp.concatenate([c, m], axis=-1) @ a_w_out[i]
            x = x + rmsnorm(y, a_norm_post[i])
        else:
            j = layer - N_A
            lambda_init = 0.8 - 0.6 * math.exp(-0.3 * layer)
            lq1, lk1, lq2, lk2 = b_lambda[j, 0], b_lambda[j, 1], b_lambda[j, 2], b_lambda[j, 3]
            lam = (jnp.exp(jnp.sum(lq1.astype(jnp.float32) * lk1.astype(jnp.float32)))
                   - jnp.exp(jnp.sum(lq2.astype(jnp.float32) * lk2.astype(jnp.float32)))
                   + lambda_init)
            h = rmsnorm(x, b_norm_pre[j])
            z = h @ b_w_in[j]
            q = partial_rope(z[..., :DIFF_QK].reshape(B, S, DIFF_HEADS, 2, DIFF_HEAD_DIM), cos, sin)
            o = diff_attention(q, k_sh, v_sh, lam)
            o = rmsnorm(o, b_subln[j]) * (1.0 - lambda_init)
            m = memory_attention(z[..., DIFF_QK:].reshape(B, S, MEM_HEADS, MEM_HEAD_DIM),
                                 rmsnorm(mem, b_mem_norm[j]), b_w_mem_kv[j])
            y = jnp.concatenate([o.reshape(B, S, DIFF_V), m], axis=-1) @ b_w_out[j]
            x = x + rmsnorm(y, b_norm_post[j])
        x = x + rmsnorm(sq_relu_mlp(rmsnorm(x, mlp_norm_pre[layer]), w_up[layer], w_down[layer]),
                        mlp_norm_post[layer])
        if layer == N_A - 1:
            kv = rmsnorm(x, kv_norm) @ w_kv
            k_sh = partial_rope(kv[..., :DIFF_QK].reshape(B, S, DIFF_HEADS, 2, DIFF_HEAD_DIM), cos, sin)
            v_sh = kv[..., DIFF_QK:].reshape(B, S, DIFF_HEADS, 2 * DIFF_HEAD_DIM)
    return x
```

```python
import functools
import math

import jax
import jax.numpy as jnp
from jax import lax
from jax.experimental import pallas as pl
from jax.experimental.pallas import tpu as pltpu

F32 = jnp.float32
BF16 = jnp.bfloat16

EPS = 1e-6
ROPE_THETA = 500000.0
LANES = 128
MEM_HEADS = 4
DIFF_HEAD_DIM = 128
ROT_DIM = DIFF_HEAD_DIM // 4
CONV_HALO = 32
VMEM_LIMIT = 56 * 1024 * 1024
LOG2E = 1.4426950408889634


def _params(*sem):
    return pltpu.CompilerParams(dimension_semantics=sem, vmem_limit_bytes=VMEM_LIMIT)


def _rms_scale(t):
    return lax.rsqrt(jnp.mean(t * t, axis=-1, keepdims=True) + EPS)


def _rmsnorm_kernel(x_ref, g_ref, o_ref):
    x = x_ref[...]
    o_ref[...] = ((x * _rms_scale(x)) * g_ref[...]).astype(o_ref.dtype)


def rmsnorm_bf16(x, g, tm):
    m, d = x.shape
    return pl.pallas_call(
        _rmsnorm_kernel,
        grid=(m // tm,),
        in_specs=[pl.BlockSpec((tm, d), lambda i: (i, 0)), pl.BlockSpec((1, d), lambda i: (0, 0))],
        out_specs=pl.BlockSpec((tm, d), lambda i: (i, 0)),
        out_shape=jax.ShapeDtypeStruct((m, d), BF16),
        compiler_params=_params("parallel"),
        name="rmsnorm",
    )(x, g.reshape(1, d))


def _rope_table_kernel(pos_ref, freq_ref, c_ref, sa_ref, sb_ref):
    ang = pos_ref[...] * freq_ref[...]
    lane = lax.broadcasted_iota(jnp.int32, ang.shape, 1)
    sin = jnp.sin(ang)
    c_ref[...] = jnp.where(lane < ROT_DIM, jnp.cos(ang), 1.0)
    sa_ref[...] = jnp.where(lane < ROT_DIM // 2, -sin, 0.0)
    sb_ref[...] = jnp.where((lane >= ROT_DIM // 2) & (lane < ROT_DIM), sin, 0.0)


def rope_tables(positions, tm):
    t = positions.size
    pos = jnp.broadcast_to(positions.reshape(t, 1).astype(F32), (t, LANES))
    inv_freq = jnp.power(jnp.float32(ROPE_THETA), -jnp.arange(0, ROT_DIM, 2, dtype=F32) / ROT_DIM)
    freq = jnp.concatenate([inv_freq, inv_freq, jnp.zeros((LANES - ROT_DIM,), F32)]).reshape(1, LANES)
    spec = pl.BlockSpec((tm, LANES), lambda i: (i, 0))
    shp = jax.ShapeDtypeStruct((t, LANES), F32)
    return pl.pallas_call(
        _rope_table_kernel,
        grid=(t // tm,),
        in_specs=[spec, pl.BlockSpec((1, LANES), lambda i: (0, 0))],
        out_specs=[spec, spec, spec],
        out_shape=[shp, shp, shp],
        compiler_params=_params("parallel"),
        name="rope_tables",
    )(pos, freq)


def _mm_kernel(a_ref, w_ref, o_ref, *, scale):
    acc = jnp.dot(a_ref[...], w_ref[...], preferred_element_type=F32)
    if scale != 1.0:
        acc = acc * scale
    o_ref[...] = acc.astype(o_ref.dtype)


def matmul(a, w, tm, tn, scale=1.0, out_dtype=BF16):
    m, k = a.shape
    n = w.shape[1]
    tn = _tile(n, tn)
    return pl.pallas_call(
        functools.partial(_mm_kernel, scale=scale),
        grid=(m // tm, n // tn),
        in_specs=[pl.BlockSpec((tm, k), lambda i, j: (i, 0)), pl.BlockSpec((k, tn), lambda i, j: (0, j))],
        out_specs=pl.BlockSpec((tm, tn), lambda i, j: (i, j)),
        out_shape=jax.ShapeDtypeStruct((m, n), out_dtype),
        compiler_params=_params("parallel", "arbitrary"),
        name="matmul",
    )(a, w)


def _mm_rope_kernel(a_ref, w_ref, c_ref, sa_ref, sb_ref, o_ref, *, scale):
    acc = jnp.dot(a_ref[...], w_ref[...], preferred_element_type=F32)
    c = c_ref[...] * scale
    sa = sa_ref[...] * scale
    sb = sb_ref[...] * scale
    for g in range(acc.shape[1] // LANES):
        t = acc[:, g * LANES:(g + 1) * LANES]
        r = t * c + pltpu.roll(t, LANES - ROT_DIM // 2, 1) * sa + pltpu.roll(t, ROT_DIM // 2, 1) * sb
        o_ref[:, g * LANES:(g + 1) * LANES] = r.astype(o_ref.dtype)


def matmul_rope(a, w, tables, tm, tn, scale=1.0):
    m, k = a.shape
    n = w.shape[1]
    tn = _tile(n, tn)
    tab = pl.BlockSpec((tm, LANES), lambda i, j: (i, 0))
    return pl.pallas_call(
        functools.partial(_mm_rope_kernel, scale=scale),
        grid=(m // tm, n // tn),
        in_specs=[pl.BlockSpec((tm, k), lambda i, j: (i, 0)), pl.BlockSpec((k, tn), lambda i, j: (0, j)),
                  tab, tab, tab],
        out_specs=pl.BlockSpec((tm, tn), lambda i, j: (i, j)),
        out_shape=jax.ShapeDtypeStruct((m, n), BF16),
        compiler_params=_params("parallel", "arbitrary"),
        name="matmul_rope",
    )(a, w, *tables)


def _mm_glu_kernel(a_ref, wu_ref, wg_ref, o_ref):
    a = a_ref[...]
    u = jnp.dot(a, wu_ref[...], preferred_element_type=F32)
    g = jnp.dot(a, wg_ref[...], preferred_element_type=F32)
    o_ref[...] = u * jax.nn.sigmoid(g)


def matmul_glu(a, wu, wg, tm, tn):
    m, k = a.shape
    n = wu.shape[1]
    tn = _tile(n, tn)
    wspec = pl.BlockSpec((k, tn), lambda i, j: (0, j))
    return pl.pallas_call(
        _mm_glu_kernel,
        grid=(m // tm, n // tn),
        in_specs=[pl.BlockSpec((tm, k), lambda i, j: (i, 0)), wspec, wspec],
        out_specs=pl.BlockSpec((tm, tn), lambda i, j: (i, j)),
        out_shape=jax.ShapeDtypeStruct((m, n), F32),
        compiler_params=_params("parallel", "arbitrary"),
        name="matmul_glu",
    )(a, wu, wg)


def _mm2_kernel(a1_ref, a2_ref, w1_ref, w2_ref, o_ref):
    o_ref[...] = (jnp.dot(a1_ref[...], w1_ref[...], preferred_element_type=F32)
                  + jnp.dot(a2_ref[...], w2_ref[...], preferred_element_type=F32))


def matmul_concat(a1, a2, w1, w2, tm, tn):
    m, k1 = a1.shape
    k2 = a2.shape[1]
    n = w1.shape[1]
    tn = _tile(n, tn)
    return pl.pallas_call(
        _mm2_kernel,
        grid=(m // tm, n // tn),
        in_specs=[pl.BlockSpec((tm, k1), lambda i, j: (i, 0)), pl.BlockSpec((tm, k2), lambda i, j: (i, 0)),
                  pl.BlockSpec((k1, tn), lambda i, j: (0, j)), pl.BlockSpec((k2, tn), lambda i, j: (0, j))],
        out_specs=pl.BlockSpec((tm, tn), lambda i, j: (i, j)),
        out_shape=jax.ShapeDtypeStruct((m, n), F32),
        compiler_params=_params("parallel", "arbitrary"),
        name="matmul_concat",
    )(a1, a2, w1, w2)


def _mlp_kernel(h_ref, wu_ref, wd_ref, o_ref):
    @pl.when(pl.program_id(1) == 0)
    def _():
        o_ref[...] = jnp.zeros_like(o_ref)

    a = jnp.dot(h_ref[...], wu_ref[...], preferred_element_type=F32)
    a = jnp.maximum(a, 0.0)
    a = (a * a).astype(BF16)
    o_ref[...] += jnp.dot(a, wd_ref[...], preferred_element_type=F32)


def mlp(h, w_up, w_down, tm, tf):
    m, d = h.shape
    f = w_up.shape[1]
    return pl.pallas_call(
        _mlp_kernel,
        grid=(m // tm, f // tf),
        in_specs=[pl.BlockSpec((tm, d), lambda i, j: (i, 0)), pl.BlockSpec((d, tf), lambda i, j: (0, j)),
                  pl.BlockSpec((tf, d), lambda i, j: (j, 0))],
        out_specs=pl.BlockSpec((tm, d), lambda i, j: (i, 0)),
        out_shape=jax.ShapeDtypeStruct((m, d), F32),
        compiler_params=_params("parallel", "arbitrary"),
        name="mlp",
    )(h, w_up, w_down)


def _norm_residual_kernel(x_ref, y_ref, gp_ref, *rest, n_next):
    g_refs, x_out, h_outs = rest[:n_next], rest[n_next], rest[n_next + 1:]
    y = y_ref[...]
    x = x_ref[...] + (y * _rms_scale(y)) * gp_ref[...]
    x_out[...] = x
    if n_next:
        xn = x * _rms_scale(x)
        for g_ref, h_out in zip(g_refs, h_outs):
            h_out[...] = (xn * g_ref[...]).astype(h_out.dtype)


def norm_residual(x, y, g_post, g_next, tm):
    m, d = x.shape
    row = pl.BlockSpec((tm, d), lambda i: (i, 0))
    vec = pl.BlockSpec((1, d), lambda i: (0, 0))
    n_next = len(g_next)
    outs = pl.pallas_call(
        functools.partial(_norm_residual_kernel, n_next=n_next),
        grid=(m // tm,),
        in_specs=[row, row, vec] + [vec] * n_next,
        out_specs=[row] + [row] * n_next,
        out_shape=[jax.ShapeDtypeStruct((m, d), F32)] + [jax.ShapeDtypeStruct((m, d), BF16)] * n_next,
        compiler_params=_params("parallel"),
        name="norm_residual",
    )(x, y, g_post.reshape(1, d), *[g.reshape(1, d) for g in g_next])
    return outs[0], list(outs[1:])


def _conv_ln_kernel(halo_ref, cur_ref, w_ref, b_ref, g_ref, beta_ref, o_ref, buf_ref, acc_ref, *, width, rows, cw):
    ts, ch = cur_ref.shape
    first = pl.program_id(1) == 0
    buf_ref[0:CONV_HALO, :] = jnp.where(first, 0.0, halo_ref[...])
    buf_ref[CONV_HALO:, :] = cur_ref[...]
    base = CONV_HALO - (width - 1)

    def chunk(ci, carry):
        c0 = pl.multiple_of(ci * cw, cw)
        for r0 in range(0, ts, rows):
            acc = jnp.zeros((rows, cw), F32) + b_ref[:, pl.ds(c0, cw)]
            for k in range(width):
                acc = acc + buf_ref[base + r0 + k:base + r0 + k + rows, pl.ds(c0, cw)] * w_ref[k:k + 1, pl.ds(c0, cw)]
            acc_ref[r0:r0 + rows, pl.ds(c0, cw)] = acc
        return carry

    lax.fori_loop(0, ch // cw, chunk, 0)
    c = acc_ref[...]
    mu = jnp.mean(c, axis=-1, keepdims=True)
    cc = c - mu
    var = jnp.mean(cc * cc, axis=-1, keepdims=True)
    y = (cc * lax.rsqrt(var + EPS)) * g_ref[...] + beta_ref[...]
    o_ref[...] = (y * jax.nn.sigmoid(y)).astype(o_ref.dtype)


def conv_ln_silu(c0, w, b, g, beta, batch, ts, rows=32, cw=256):
    t, ch = c0.shape
    seq = t // batch
    width = w.shape[0]
    nt = seq // ts
    hb = ts // CONV_HALO
    vec = pl.BlockSpec((1, ch), lambda bi, i: (0, 0))
    return pl.pallas_call(
        functools.partial(_conv_ln_kernel, width=width, rows=rows, cw=cw),
        grid=(batch, nt),
        in_specs=[pl.BlockSpec((CONV_HALO, ch), lambda bi, i: (jnp.maximum((bi * nt + i) * hb - 1, 0), 0)),
                  pl.BlockSpec((ts, ch), lambda bi, i: (bi * nt + i, 0)),
                  pl.BlockSpec((width, ch), lambda bi, i: (0, 0)), vec, vec, vec],
        out_specs=pl.BlockSpec((ts, ch), lambda bi, i: (bi * nt + i, 0)),
        out_shape=jax.ShapeDtypeStruct((t, ch), BF16),
        scratch_shapes=[pltpu.VMEM((ts + CONV_HALO, ch), F32), pltpu.VMEM((ts, ch), F32)],
        compiler_params=_params("parallel", "arbitrary"),
        name="conv_ln_silu",
    )(c0, c0, w, b.reshape(1, ch), g.reshape(1, ch), beta.reshape(1, ch))


def _mem_attn_kernel(q_ref, kv_ref, o_ref):
    width = q_ref.shape[1]
    hd = width // MEM_HEADS
    for h in range(MEM_HEADS):
        q = q_ref[:, h * hd:(h + 1) * hd]
        mk = kv_ref[:, h * hd:(h + 1) * hd]
        mv = kv_ref[:, width + h * hd:width + (h + 1) * hd]
        s = lax.dot_general(q, mk, (((1,), (1,)), ((), ())), preferred_element_type=F32)
        p = jnp.exp(s - jnp.max(s, axis=-1, keepdims=True))
        l = jnp.sum(p, axis=-1, keepdims=True)
        o = jnp.dot(p.astype(BF16), mv, preferred_element_type=F32)
        o_ref[:, h * hd:(h + 1) * hd] = (o / l).astype(o_ref.dtype)


def mem_attention(qm, mkv, batch, tm):
    t, width = qm.shape
    nt = t // batch // tm
    mt = mkv.shape[0] // batch
    return pl.pallas_call(
        _mem_attn_kernel,
        grid=(batch, nt),
        in_specs=[pl.BlockSpec((tm, width), lambda b, i: (b * nt + i, 0)),
                  pl.BlockSpec((mt, 2 * width), lambda b, i: (b, 0))],
        out_specs=pl.BlockSpec((tm, width), lambda b, i: (b * nt + i, 0)),
        out_shape=jax.ShapeDtypeStruct((t, width), BF16),
        compiler_params=_params("parallel", "arbitrary"),
        name="mem_attention",
    )(qm, mkv)


def _diff_attn_kernel(lam_ref, g_ref, q_ref, k_ref, v_ref, o_ref, acc_ref, *, lambda_init):
    tq = q_ref.shape[0]
    d = DIFF_HEAD_DIM
    qi = pl.program_id(2)
    q = q_ref[...]
    q1, q2 = q[:, :d], q[:, d:]
    acc_ref[...] = jnp.zeros_like(acc_ref)
    row = lax.broadcasted_iota(jnp.int32, (tq, tq), 0)
    col = lax.broadcasted_iota(jnp.int32, (tq, tq), 1)
    causal = col <= row
    nt = (((1,), (1,)), ((), ()))

    def step(j, carry, masked):
        m1, l1, m2, l2 = carry
        off = pl.multiple_of(j * tq, tq)
        k = k_ref[pl.ds(off, tq), :]
        v = v_ref[pl.ds(off, tq), :]
        s1 = lax.dot_general(q1, k[:, :d], nt, preferred_element_type=F32)
        s2 = lax.dot_general(q2, k[:, d:], nt, preferred_element_type=F32)
        if masked:
            s1 = jnp.where(causal, s1, -jnp.inf)
            s2 = jnp.where(causal, s2, -jnp.inf)
        m1n = jnp.maximum(m1, jnp.max(s1, axis=-1, keepdims=True))
        m2n = jnp.maximum(m2, jnp.max(s2, axis=-1, keepdims=True))
        p1 = jnp.exp2(s1 - m1n)
        p2 = jnp.exp2(s2 - m2n)
        a1 = jnp.exp2(m1 - m1n)
        a2 = jnp.exp2(m2 - m2n)
        l1 = a1 * l1 + jnp.sum(p1, axis=-1, keepdims=True)
        l2 = a2 * l2 + jnp.sum(p2, axis=-1, keepdims=True)
        p = jnp.concatenate([p1.astype(BF16), p2.astype(BF16)], axis=0)
        pv = jnp.dot(p, v, preferred_element_type=F32)
        alpha = jnp.concatenate([a1, a2], axis=0)
        acc_ref[...] = acc_ref[...] * alpha + pv
        return m1n, l1, m2n, l2

    ninf = jnp.full((tq, 1), -jnp.inf, F32)
    zero = jnp.zeros((tq, 1), F32)
    carry = lax.fori_loop(0, qi, functools.partial(step, masked=False), (ninf, zero, ninf, zero))
    _, l1, _, l2 = step(qi, carry, True)

    lm = lam_ref[...]
    lam = (jnp.exp(jnp.sum(lm[0:1] * lm[1:2], axis=-1, keepdims=True))
           - jnp.exp(jnp.sum(lm[2:3] * lm[3:4], axis=-1, keepdims=True)) + lambda_init)
    o = acc_ref[0:tq, :] / l1 - lam * (acc_ref[tq:, :] / l2)
    o = (o * _rms_scale(o)) * g_ref[...] * (1.0 - lambda_init)
    o_ref[...] = o.astype(o_ref.dtype)


def diff_attention(q, k, v, lam_params, subln, batch, tq, lambda_init):
    t, width = q.shape
    seq = t // batch
    hw = 2 * DIFF_HEAD_DIM
    heads = width // hw
    nq = seq // tq
    return pl.pallas_call(
        functools.partial(_diff_attn_kernel, lambda_init=lambda_init),
        grid=(batch, heads, nq),
        in_specs=[pl.BlockSpec((4, DIFF_HEAD_DIM), lambda b, h, i: (0, 0)),
                  pl.BlockSpec((1, hw), lambda b, h, i: (0, 0)),
                  pl.BlockSpec((tq, hw), lambda b, h, i: (b * nq + i, h)),
                  pl.BlockSpec((seq, hw), lambda b, h, i: (b, h)),
                  pl.BlockSpec((seq, hw), lambda b, h, i: (b, h))],
        out_specs=pl.BlockSpec((tq, hw), lambda b, h, i: (b * nq + i, h)),
        out_shape=jax.ShapeDtypeStruct((t, width), BF16),
        scratch_shapes=[pltpu.VMEM((2 * tq, hw), F32)],
        compiler_params=_params("parallel", "parallel", "arbitrary"),
        name="diff_attention",
    )(lam_params, subln.reshape(1, hw), q, k, v)


def _tile(n, want):
    t = min(n, want)
    while n % t:
        t //= 2
    return t


def kernel(x, mem, positions, a_norm_pre, a_norm_post, a_mem_norm, a_w_in, a_conv_w, a_conv_b, a_ln_g, a_ln_b,
           a_w_mem_kv, a_w_out, kv_norm, w_kv, b_norm_pre, b_norm_post, b_mem_norm, b_w_in, b_lambda, b_subln,
           b_w_mem_kv, b_w_out, mlp_norm_pre, mlp_norm_post, w_up, w_down):
    batch, seq, d = x.shape
    t = batch * seq
    n_a = a_norm_pre.shape[0]
    depth = n_a + b_norm_pre.shape[0]
    conv_ch = a_conv_w.shape[2]
    mem_width = a_w_in.shape[2] - 2 * conv_ch
    qk_width = b_w_in.shape[2] - mem_width
    mem_scale = float((mem_width // MEM_HEADS) ** -0.5)

    tm = _tile(t, 1024)
    tn = 512
    te = _tile(t, 256)
    mem2 = mem.reshape(-1, d)
    tmem = _tile(mem2.shape[0], 512)

    xr = x.reshape(t, d)
    tables = rope_tables(positions, _tile(t, 1024))
    h = rmsnorm_bf16(xr, a_norm_pre[0] if n_a else b_norm_pre[0], te)
    k_sh = v_sh = None

    for layer in range(depth):
        last = layer == depth - 1
        if layer < n_a:
            i = layer
            w_in = a_w_in[i].astype(BF16)
            c0 = matmul_glu(h, w_in[:, :conv_ch], w_in[:, conv_ch:2 * conv_ch], tm, tn)
            qm = matmul(h, w_in[:, 2 * conv_ch:], tm, tn, scale=mem_scale)
            main = conv_ln_silu(c0, a_conv_w[i], a_conv_b[i], a_ln_g[i], a_ln_b[i], batch, _tile(seq, 128))
            mem_n = rmsnorm_bf16(mem2, a_mem_norm[i], _tile(mem2.shape[0], 256))
            mkv = matmul(mem_n, a_w_mem_kv[i].astype(BF16), tmem, tn)
            w_out = a_w_out[i].astype(BF16)
            g_post = a_norm_post[i]
        else:
            j = layer - n_a
            lambda_init = 0.8 - 0.6 * math.exp(-0.3 * layer)
            w_in = b_w_in[j].astype(BF16)
            q = matmul_rope(h, w_in[:, :qk_width], tables, tm, tn, scale=DIFF_HEAD_DIM ** -0.5 * LOG2E)
            qm = matmul(h, w_in[:, qk_width:], tm, tn, scale=mem_scale)
            main = diff_attention(q, k_sh, v_sh, b_lambda[j], b_subln[j], batch, _tile(seq, 512), lambda_init)
            mem_n = rmsnorm_bf16(mem2, b_mem_norm[j], _tile(mem2.shape[0], 256))
            mkv = matmul(mem_n, b_w_mem_kv[j].astype(BF16), tmem, tn)
            w_out = b_w_out[j].astype(BF16)
            g_post = b_norm_post[j]

        m_out = mem_attention(qm, mkv, batch, _tile(seq, 512))
        kmain = main.shape[1]
        y = matmul_concat(main, m_out, w_out[:kmain], w_out[kmain:], tm, tn)
        xr, (h,) = norm_residual(xr, y, g_post, [mlp_norm_pre[layer]], te)

        y = mlp(h, w_up[layer].astype(BF16), w_down[layer].astype(BF16), _tile(t, 512), 512)
        g_next = []
        if layer == n_a - 1:
            g_next.append(kv_norm)
        if not last:
            g_next.append(a_norm_pre[layer + 1] if layer + 1 < n_a else b_norm_pre[layer + 1 - n_a])
        xr, hs = norm_residual(xr, y, mlp_norm_post[layer], g_next, te)
        if layer == n_a - 1:
            w_kv_b = w_kv.astype(BF16)
            k_sh = matmul_rope(hs[0], w_kv_b[:, :qk_width], tables, tm, tn)
            v_sh = matmul(hs[0], w_kv_b[:, qk_width:], tm, tn)
        if not last:
            h = hs[-1]

    return xr.reshape(batch, seq, d)
```

```python
import functools
import math

import jax
import jax.numpy as jnp
from jax import lax
from jax.experimental import pallas as pl
from jax.experimental.pallas import tpu as pltpu

F32 = jnp.float32
BF16 = jnp.bfloat16

EPS = 1e-6
ROPE_THETA = 500000.0
LANES = 128
MEM_HEADS = 4
DIFF_HEAD_DIM = 128
ROT_DIM = DIFF_HEAD_DIM // 4
CONV_HALO = 32
VMEM_LIMIT = 56 * 1024 * 1024
LOG2E = 1.4426950408889634


def _params(*sem):
    return pltpu.CompilerParams(dimension_semantics=sem, vmem_limit_bytes=VMEM_LIMIT)


def _rms_scale(t):
    return lax.rsqrt(jnp.mean(t * t, axis=-1, keepdims=True) + EPS)


def _rmsnorm_kernel(x_ref, g_ref, o_ref):
    x = x_ref[...]
    o_ref[...] = ((x * _rms_scale(x)) * g_ref[...]).astype(o_ref.dtype)


def rmsnorm_bf16(x, g, tm):
    m, d = x.shape
    return pl.pallas_call(
        _rmsnorm_kernel,
        grid=(m // tm,),
        in_specs=[pl.BlockSpec((tm, d), lambda i: (i, 0)), pl.BlockSpec((1, d), lambda i: (0, 0))],
        out_specs=pl.BlockSpec((tm, d), lambda i: (i, 0)),
        out_shape=jax.ShapeDtypeStruct((m, d), BF16),
        compiler_params=_params("parallel"),
        name="rmsnorm",
    )(x, g.reshape(1, d))


def _rope_table_kernel(pos_ref, freq_ref, c_ref, sa_ref, sb_ref):
    ang = pos_ref[...] * freq_ref[...]
    lane = lax.broadcasted_iota(jnp.int32, ang.shape, 1)
    sin = jnp.sin(ang)
    c_ref[...] = jnp.where(lane < ROT_DIM, jnp.cos(ang), 1.0)
    sa_ref[...] = jnp.where(lane < ROT_DIM // 2, -sin, 0.0)
    sb_ref[...] = jnp.where((lane >= ROT_DIM // 2) & (lane < ROT_DIM), sin, 0.0)


def rope_tables(positions, tm):
    t = positions.size
    pos = jnp.broadcast_to(positions.reshape(t, 1).astype(F32), (t, LANES))
    inv_freq = jnp.power(jnp.float32(ROPE_THETA), -jnp.arange(0, ROT_DIM, 2, dtype=F32) / ROT_DIM)
    freq = jnp.concatenate([inv_freq, inv_freq, jnp.zeros((LANES - ROT_DIM,), F32)]).reshape(1, LANES)
    spec = pl.BlockSpec((tm, LANES), lambda i: (i, 0))
    shp = jax.ShapeDtypeStruct((t, LANES), F32)
    return pl.pallas_call(
        _rope_table_kernel,
        grid=(t // tm,),
        in_specs=[spec, pl.BlockSpec((1, LANES), lambda i: (0, 0))],
        out_specs=[spec, spec, spec],
        out_shape=[shp, shp, shp],
        compiler_params=_params("parallel"),
        name="rope_tables",
    )(pos, freq)


def _mm_kernel(a_ref, w_ref, o_ref, *, scale):
    acc = jnp.dot(a_ref[...], w_ref[...], preferred_element_type=F32)
    if scale != 1.0:
        acc = acc * scale
    o_ref[...] = acc.astype(o_ref.dtype)


def matmul(a, w, tm, tn, scale=1.0, out_dtype=BF16):
    m, k = a.shape
    n = w.shape[1]
    tn = _tile(n, tn)
    return pl.pallas_call(
        functools.partial(_mm_kernel, scale=scale),
        grid=(m // tm, n // tn),
        in_specs=[pl.BlockSpec((tm, k), lambda i, j: (i, 0)), pl.BlockSpec((k, tn), lambda i, j: (0, j))],
        out_specs=pl.BlockSpec((tm, tn), lambda i, j: (i, j)),
        out_shape=jax.ShapeDtypeStruct((m, n), out_dtype),
        compiler_params=_params("parallel", "arbitrary"),
        name="matmul",
    )(a, w)


def _mm_nt_kernel(wt_ref, a_ref, o_ref):
    acc = lax.dot_general(wt_ref[...], a_ref[...], (((1,), (1,)), ((), ())), preferred_element_type=F32)
    o_ref[...] = acc.astype(o_ref.dtype)


def matmul_transposed_out(a, wt, tm, tn):
    m, k = a.shape
    n = wt.shape[0]
    tn = _tile(n, tn)
    return pl.pallas_call(
        _mm_nt_kernel,
        grid=(m // tm, n // tn),
        in_specs=[pl.BlockSpec((tn, k), lambda i, j: (j, 0)), pl.BlockSpec((tm, k), lambda i, j: (i, 0))],
        out_specs=pl.BlockSpec((tn, tm), lambda i, j: (j, i)),
        out_shape=jax.ShapeDtypeStruct((n, m), BF16),
        compiler_params=_params("parallel", "arbitrary"),
        name="matmul_transposed_out",
    )(wt, a)


def _mm_rope_kernel(a_ref, w_ref, c_ref, sa_ref, sb_ref, o_ref, *, scale):
    acc = jnp.dot(a_ref[...], w_ref[...], preferred_element_type=F32)
    c = c_ref[...] * scale
    sa = sa_ref[...] * scale
    sb = sb_ref[...] * scale
    for g in range(acc.shape[1] // LANES):
        t = acc[:, g * LANES:(g + 1) * LANES]
        r = t * c + pltpu.roll(t, LANES - ROT_DIM // 2, 1) * sa + pltpu.roll(t, ROT_DIM // 2, 1) * sb
        o_ref[:, g * LANES:(g + 1) * LANES] = r.astype(o_ref.dtype)


def matmul_rope(a, w, tables, tm, tn, scale=1.0):
    m, k = a.shape
    n = w.shape[1]
    tn = _tile(n, tn)
    tab = pl.BlockSpec((tm, LANES), lambda i, j: (i, 0))
    return pl.pallas_call(
        functools.partial(_mm_rope_kernel, scale=scale),
        grid=(m // tm, n // tn),
        in_specs=[pl.BlockSpec((tm, k), lambda i, j: (i, 0)), pl.BlockSpec((k, tn), lambda i, j: (0, j)),
                  tab, tab, tab],
        out_specs=pl.BlockSpec((tm, tn), lambda i, j: (i, j)),
        out_shape=jax.ShapeDtypeStruct((m, n), BF16),
        compiler_params=_params("parallel", "arbitrary"),
        name="matmul_rope",
    )(a, w, *tables)


def _mm_glu_kernel(a_ref, wu_ref, wg_ref, o_ref):
    a = a_ref[...]
    u = jnp.dot(a, wu_ref[...], preferred_element_type=F32)
    g = jnp.dot(a, wg_ref[...], preferred_element_type=F32)
    o_ref[...] = u * jax.nn.sigmoid(g)


def matmul_glu(a, wu, wg, tm, tn):
    m, k = a.shape
    n = wu.shape[1]
    tn = _tile(n, tn)
    wspec = pl.BlockSpec((k, tn), lambda i, j: (0, j))
    return pl.pallas_call(
        _mm_glu_kernel,
        grid=(m // tm, n // tn),
        in_specs=[pl.BlockSpec((tm, k), lambda i, j: (i, 0)), wspec, wspec],
        out_specs=pl.BlockSpec((tm, tn), lambda i, j: (i, j)),
        out_shape=jax.ShapeDtypeStruct((m, n), F32),
        compiler_params=_params("parallel", "arbitrary"),
        name="matmul_glu",
    )(a, wu, wg)


def _mm2_kernel(a1_ref, a2_ref, w1_ref, w2_ref, o_ref):
    o_ref[...] = (jnp.dot(a1_ref[...], w1_ref[...], preferred_element_type=F32)
                  + jnp.dot(a2_ref[...], w2_ref[...], preferred_element_type=F32))


def matmul_concat(a1, a2, w1, w2, tm, tn):
    m, k1 = a1.shape
    k2 = a2.shape[1]
    n = w1.shape[1]
    tn = _tile(n, tn)
    return pl.pallas_call(
        _mm2_kernel,
        grid=(m // tm, n // tn),
        in_specs=[pl.BlockSpec((tm, k1), lambda i, j: (i, 0)), pl.BlockSpec((tm, k2), lambda i, j: (i, 0)),
                  pl.BlockSpec((k1, tn), lambda i, j: (0, j)), pl.BlockSpec((k2, tn), lambda i, j: (0, j))],
        out_specs=pl.BlockSpec((tm, tn), lambda i, j: (i, j)),
        out_shape=jax.ShapeDtypeStruct((m, n), F32),
        compiler_params=_params("parallel", "arbitrary"),
        name="matmul_concat",
    )(a1, a2, w1, w2)


def _mlp_kernel(h_ref, wu_ref, wd_ref, o_ref):
    @pl.when(pl.program_id(1) == 0)
    def _():
        o_ref[...] = jnp.zeros_like(o_ref)

    a = jnp.dot(h_ref[...], wu_ref[...], preferred_element_type=F32)
    a = jnp.maximum(a, 0.0)
    a = (a * a).astype(BF16)
    o_ref[...] += jnp.dot(a, wd_ref[...], preferred_element_type=F32)


def mlp(h, w_up, w_down, tm, tf):
    m, d = h.shape
    f = w_up.shape[1]
    return pl.pallas_call(
        _mlp_kernel,
        grid=(m // tm, f // tf),
        in_specs=[pl.BlockSpec((tm, d), lambda i, j: (i, 0)), pl.BlockSpec((d, tf), lambda i, j: (0, j)),
                  pl.BlockSpec((tf, d), lambda i, j: (j, 0))],
        out_specs=pl.BlockSpec((tm, d), lambda i, j: (i, 0)),
        out_shape=jax.ShapeDtypeStruct((m, d), F32),
        compiler_params=_params("parallel", "arbitrary"),
        name="mlp",
    )(h, w_up, w_down)


def _norm_residual_kernel(x_ref, y_ref, gp_ref, *rest, n_next):
    g_refs, x_out, h_outs = rest[:n_next], rest[n_next], rest[n_next + 1:]
    y = y_ref[...]
    x = x_ref[...] + (y * _rms_scale(y)) * gp_ref[...]
    x_out[...] = x
    if n_next:
        xn = x * _rms_scale(x)
        for g_ref, h_out in zip(g_refs, h_outs):
            h_out[...] = (xn * g_ref[...]).astype(h_out.dtype)


def norm_residual(x, y, g_post, g_next, tm):
    m, d = x.shape
    row = pl.BlockSpec((tm, d), lambda i: (i, 0))
    vec = pl.BlockSpec((1, d), lambda i: (0, 0))
    n_next = len(g_next)
    outs = pl.pallas_call(
        functools.partial(_norm_residual_kernel, n_next=n_next),
        grid=(m // tm,),
        in_specs=[row, row, vec] + [vec] * n_next,
        out_specs=[row] + [row] * n_next,
        out_shape=[jax.ShapeDtypeStruct((m, d), F32)] + [jax.ShapeDtypeStruct((m, d), BF16)] * n_next,
        compiler_params=_params("parallel"),
        name="norm_residual",
    )(x, y, g_post.reshape(1, d), *[g.reshape(1, d) for g in g_next])
    return outs[0], list(outs[1:])


def _conv_ln_kernel(halo_ref, cur_ref, w_ref, b_ref, g_ref, beta_ref, o_ref, buf_ref, acc_ref, *, width, rows, cw):
    ts, ch = cur_ref.shape
    first = pl.program_id(1) == 0
    buf_ref[0:CONV_HALO, :] = jnp.where(first, 0.0, halo_ref[...])
    buf_ref[CONV_HALO:, :] = cur_ref[...]
    base = CONV_HALO - (width - 1)

    def chunk(ci, carry):
        c0 = pl.multiple_of(ci * cw, cw)
        for r0 in range(0, ts, rows):
            acc = jnp.zeros((rows, cw), F32) + b_ref[:, pl.ds(c0, cw)]
            for k in range(width):
                acc = acc + buf_ref[base + r0 + k:base + r0 + k + rows, pl.ds(c0, cw)] * w_ref[k:k + 1, pl.ds(c0, cw)]
            acc_ref[r0:r0 + rows, pl.ds(c0, cw)] = acc
        return carry

    lax.fori_loop(0, ch // cw, chunk, 0)
    c = acc_ref[...]
    mu = jnp.mean(c, axis=-1, keepdims=True)
    cc = c - mu
    var = jnp.mean(cc * cc, axis=-1, keepdims=True)
    y = (cc * lax.rsqrt(var + EPS)) * g_ref[...] + beta_ref[...]
    o_ref[...] = (y * jax.nn.sigmoid(y)).astype(o_ref.dtype)


def conv_ln_silu(c0, w, b, g, beta, batch, ts, rows=32, cw=256):
    t, ch = c0.shape
    seq = t // batch
    width = w.shape[0]
    nt = seq // ts
    hb = ts // CONV_HALO
    vec = pl.BlockSpec((1, ch), lambda bi, i: (0, 0))
    return pl.pallas_call(
        functools.partial(_conv_ln_kernel, width=width, rows=rows, cw=cw),
        grid=(batch, nt),
        in_specs=[pl.BlockSpec((CONV_HALO, ch), lambda bi, i: (jnp.maximum((bi * nt + i) * hb - 1, 0), 0)),
                  pl.BlockSpec((ts, ch), lambda bi, i: (bi * nt + i, 0)),
                  pl.BlockSpec((width, ch), lambda bi, i: (0, 0)), vec, vec, vec],
        out_specs=pl.BlockSpec((ts, ch), lambda bi, i: (bi * nt + i, 0)),
        out_shape=jax.ShapeDtypeStruct((t, ch), BF16),
        scratch_shapes=[pltpu.VMEM((ts + CONV_HALO, ch), F32), pltpu.VMEM((ts, ch), F32)],
        compiler_params=_params("parallel", "arbitrary"),
        name="conv_ln_silu",
    )(c0, c0, w, b.reshape(1, ch), g.reshape(1, ch), beta.reshape(1, ch))


def _mem_attn_kernel(q_ref, kv_ref, o_ref):
    width = q_ref.shape[1]
    hd = width // MEM_HEADS
    for h in range(MEM_HEADS):
        q = q_ref[:, h * hd:(h + 1) * hd]
        mk = kv_ref[:, h * hd:(h + 1) * hd]
        mv = kv_ref[:, width + h * hd:width + (h + 1) * hd]
        s = lax.dot_general(q, mk, (((1,), (1,)), ((), ())), preferred_element_type=F32)
        p = jnp.exp(s - jnp.max(s, axis=-1, keepdims=True))
        l = jnp.sum(p, axis=-1, keepdims=True)
        o = jnp.dot(p.astype(BF16), mv, preferred_element_type=F32)
        o_ref[:, h * hd:(h + 1) * hd] = (o / l).astype(o_ref.dtype)


def mem_attention(qm, mkv, batch, tm):
    t, width = qm.shape
    nt = t // batch // tm
    mt = mkv.shape[0] // batch
    return pl.pallas_call(
        _mem_attn_kernel,
        grid=(batch, nt),
        in_specs=[pl.BlockSpec((tm, width), lambda b, i: (b * nt + i, 0)),
                  pl.BlockSpec((mt, 2 * width), lambda b, i: (b, 0))],
        out_specs=pl.BlockSpec((tm, width), lambda b, i: (b * nt + i, 0)),
        out_shape=jax.ShapeDtypeStruct((t, width), BF16),
        compiler_params=_params("parallel", "arbitrary"),
        name="mem_attention",
    )(qm, mkv)


def _diff_attn_kernel(lam_ref, g_ref, q_ref, k_ref, vt_ref, o_ref, acc1_ref, acc2_ref, *, lambda_init):
    tq = q_ref.shape[0]
    d = DIFF_HEAD_DIM
    qi = pl.program_id(2)
    q = q_ref[...]
    qs = (q[:, :d], q[:, d:])
    accs = (acc1_ref, acc2_ref)
    for acc in accs:
        acc[...] = jnp.zeros_like(acc)
    key = lax.broadcasted_iota(jnp.int32, (tq, tq), 0)
    qry = lax.broadcasted_iota(jnp.int32, (tq, tq), 1)
    causal = key <= qry
    nt = (((1,), (1,)), ((), ()))

    def step(j, carry, masked):
        off = pl.multiple_of(j * tq, tq)
        k = k_ref[pl.ds(off, tq), :]
        vt = vt_ref[:, pl.ds(off, tq)]
        out = []
        for c in range(2):
            m, l = carry[2 * c], carry[2 * c + 1]
            s = lax.dot_general(k[:, c * d:(c + 1) * d], qs[c], nt, preferred_element_type=F32)
            if masked:
                s = jnp.where(causal, s, -jnp.inf)
            mn = jnp.maximum(m, jnp.max(s, axis=0, keepdims=True))
            p = jnp.exp2(s - mn)
            alpha = jnp.exp2(m - mn)
            l = alpha * l + jnp.sum(p, axis=0, keepdims=True)
            accs[c][...] = accs[c][...] * alpha + jnp.dot(vt, p.astype(BF16), preferred_element_type=F32)
            out += [mn, l]
        return tuple(out)

    ninf = jnp.full((1, tq), -jnp.inf, F32)
    zero = jnp.zeros((1, tq), F32)
    carry = lax.fori_loop(0, qi, functools.partial(step, masked=False), (ninf, zero, ninf, zero))
    _, l1, _, l2 = step(qi, carry, True)

    lm = lam_ref[...]
    lam = (jnp.exp(jnp.sum(lm[0:1] * lm[1:2], axis=-1, keepdims=True))
           - jnp.exp(jnp.sum(lm[2:3] * lm[3:4], axis=-1, keepdims=True)) + lambda_init)
    o = acc1_ref[...] / l1 - lam * (acc2_ref[...] / l2)
    inv = lax.rsqrt(jnp.mean(o * o, axis=0, keepdims=True) + EPS)
    o = (o * inv) * g_ref[...] * (1.0 - lambda_init)
    o_ref[...] = o.T.astype(o_ref.dtype)


def diff_attention(q, k, vt, lam_params, subln, batch, tq, lambda_init):
    t, width = q.shape
    seq = t // batch
    hw = 2 * DIFF_HEAD_DIM
    heads = width // hw
    nq = seq // tq
    return pl.pallas_call(
        functools.partial(_diff_attn_kernel, lambda_init=lambda_init),
        grid=(batch, heads, nq),
        in_specs=[pl.BlockSpec((4, DIFF_HEAD_DIM), lambda b, h, i: (0, 0)),
                  pl.BlockSpec((hw, 1), lambda b, h, i: (0, 0)),
                  pl.BlockSpec((tq, hw), lambda b, h, i: (b * nq + i, h)),
                  pl.BlockSpec((seq, hw), lambda b, h, i: (b, h)),
                  pl.BlockSpec((hw, seq), lambda b, h, i: (h, b))],
        out_specs=pl.BlockSpec((tq, hw), lambda b, h, i: (b * nq + i, h)),
        out_shape=jax.ShapeDtypeStruct((t, width), BF16),
        scratch_shapes=[pltpu.VMEM((hw, tq), F32), pltpu.VMEM((hw, tq), F32)],
        compiler_params=_params("parallel", "parallel", "arbitrary"),
        name="diff_attention",
    )(lam_params, subln.reshape(hw, 1), q, k, vt)


def _tile(n, want):
    t = min(n, want)
    while n % t:
        t //= 2
    return t


def kernel(x, mem, positions, a_norm_pre, a_norm_post, a_mem_norm, a_w_in, a_conv_w, a_conv_b, a_ln_g, a_ln_b,
           a_w_mem_kv, a_w_out, kv_norm, w_kv, b_norm_pre, b_norm_post, b_mem_norm, b_w_in, b_lambda, b_subln,
           b_w_mem_kv, b_w_out, mlp_norm_pre, mlp_norm_post, w_up, w_down):
    batch, seq, d = x.shape
    t = batch * seq
    n_a = a_norm_pre.shape[0]
    depth = n_a + b_norm_pre.shape[0]
    conv_ch = a_conv_w.shape[2]
    mem_width = a_w_in.shape[2] - 2 * conv_ch
    qk_width = b_w_in.shape[2] - mem_width
    mem_scale = float((mem_width // MEM_HEADS) ** -0.5)

    tm = _tile(t, 1024)
    tn = 512
    te = _tile(t, 256)
    mem2 = mem.reshape(-1, d)
    tmem = _tile(mem2.shape[0], 512)

    xr = x.reshape(t, d)
    tables = rope_tables(positions, _tile(t, 1024))
    h = rmsnorm_bf16(xr, a_norm_pre[0] if n_a else b_norm_pre[0], te)
    k_sh = v_sh = None

    for layer in range(depth):
        last = layer == depth - 1
        if layer < n_a:
            i = layer
            w_in = a_w_in[i].astype(BF16)
            c0 = matmul_glu(h, w_in[:, :conv_ch], w_in[:, conv_ch:2 * conv_ch], tm, tn)
            qm = matmul(h, w_in[:, 2 * conv_ch:], tm, tn, scale=mem_scale)
            main = conv_ln_silu(c0, a_conv_w[i], a_conv_b[i], a_ln_g[i], a_ln_b[i], batch, _tile(seq, 128))
            mem_n = rmsnorm_bf16(mem2, a_mem_norm[i], _tile(mem2.shape[0], 256))
            mkv = matmul(mem_n, a_w_mem_kv[i].astype(BF16), tmem, tn)
            w_out = a_w_out[i].astype(BF16)
            g_post = a_norm_post[i]
        else:
            j = layer - n_a
            lambda_init = 0.8 - 0.6 * math.exp(-0.3 * layer)
            w_in = b_w_in[j].astype(BF16)
            q = matmul_rope(h, w_in[:, :qk_width], tables, tm, tn, scale=DIFF_HEAD_DIM ** -0.5 * LOG2E)
            qm = matmul(h, w_in[:, qk_width:], tm, tn, scale=mem_scale)
            main = diff_attention(q, k_sh, v_sh, b_lambda[j], b_subln[j], batch, _tile(seq, 512), lambda_init)
            mem_n = rmsnorm_bf16(mem2, b_mem_norm[j], _tile(mem2.shape[0], 256))
            mkv = matmul(mem_n, b_w_mem_kv[j].astype(BF16), tmem, tn)
            w_out = b_w_out[j].astype(BF16)
            g_post = b_norm_post[j]

        m_out = mem_attention(qm, mkv, batch, _tile(seq, 512))
        kmain = main.shape[1]
        y = matmul_concat(main, m_out, w_out[:kmain], w_out[kmain:], tm, tn)
        xr, (h,) = norm_residual(xr, y, g_post, [mlp_norm_pre[layer]], te)

        y = mlp(h, w_up[layer].astype(BF16), w_down[layer].astype(BF16), _tile(t, 512), 512)
        g_next = []
        if layer == n_a - 1:
            g_next.append(kv_norm)
        if not last:
            g_next.append(a_norm_pre[layer + 1] if layer + 1 < n_a else b_norm_pre[layer + 1 - n_a])
        xr, hs = norm_residual(xr, y, mlp_norm_post[layer], g_next, te)
        if layer == n_a - 1:
            w_kv_b = w_kv.astype(BF16)
            k_sh = matmul_rope(hs[0], w_kv_b[:, :qk_width], tables, tm, tn)
            v_sh = matmul_transposed_out(hs[0], w_kv[:, qk_width:].T.astype(BF16), tm, tn)
        if not last:
            h = hs[-1]

    return xr.reshape(batch, seq, d)
```

```python
import functools
import math

import jax
import jax.numpy as jnp
from jax import lax
from jax.experimental import pallas as pl
from jax.experimental.pallas import tpu as pltpu

F32 = jnp.float32
BF16 = jnp.bfloat16

EPS = 1e-6
ROPE_THETA = 500000.0
LANES = 128
MEM_HEADS = 4
DIFF_HEAD_DIM = 128
ROT_DIM = DIFF_HEAD_DIM // 4
LN_ROWS = 16
CONV_HALO = 32
VMEM_LIMIT = 56 * 1024 * 1024
LOG2E = 1.4426950408889634


def _params(*sem):
    return pltpu.CompilerParams(dimension_semantics=sem, vmem_limit_bytes=VMEM_LIMIT)


def _rms_scale(t):
    return lax.rsqrt(jnp.mean(t * t, axis=-1, keepdims=True) + EPS)


def _rmsnorm_kernel(x_ref, g_ref, o_ref):
    x = x_ref[...]
    o_ref[...] = ((x * _rms_scale(x)) * g_ref[...]).astype(o_ref.dtype)


def rmsnorm_bf16(x, g, tm):
    m, d = x.shape
    return pl.pallas_call(
        _rmsnorm_kernel,
        grid=(m // tm,),
        in_specs=[pl.BlockSpec((tm, d), lambda i: (i, 0)), pl.BlockSpec((1, d), lambda i: (0, 0))],
        out_specs=pl.BlockSpec((tm, d), lambda i: (i, 0)),
        out_shape=jax.ShapeDtypeStruct((m, d), BF16),
        compiler_params=_params("parallel"),
        name="rmsnorm",
    )(x, g.reshape(1, d))


def _rope_table_kernel(pos_ref, freq_ref, c_ref, sa_ref, sb_ref):
    ang = pos_ref[...] * freq_ref[...]
    lane = lax.broadcasted_iota(jnp.int32, ang.shape, 1)
    sin = jnp.sin(ang)
    c_ref[...] = jnp.where(lane < ROT_DIM, jnp.cos(ang), 1.0)
    sa_ref[...] = jnp.where(lane < ROT_DIM // 2, -sin, 0.0)
    sb_ref[...] = jnp.where((lane >= ROT_DIM // 2) & (lane < ROT_DIM), sin, 0.0)


def rope_tables(positions, tm):
    t = positions.size
    pos = jnp.broadcast_to(positions.reshape(t, 1).astype(F32), (t, LANES))
    inv_freq = jnp.power(jnp.float32(ROPE_THETA), -jnp.arange(0, ROT_DIM, 2, dtype=F32) / ROT_DIM)
    freq = jnp.concatenate([inv_freq, inv_freq, jnp.zeros((LANES - ROT_DIM,), F32)]).reshape(1, LANES)
    spec = pl.BlockSpec((tm, LANES), lambda i: (i, 0))
    shp = jax.ShapeDtypeStruct((t, LANES), F32)
    return pl.pallas_call(
        _rope_table_kernel,
        grid=(t // tm,),
        in_specs=[spec, pl.BlockSpec((1, LANES), lambda i: (0, 0))],
        out_specs=[spec, spec, spec],
        out_shape=[shp, shp, shp],
        compiler_params=_params("parallel"),
        name="rope_tables",
    )(pos, freq)


def _mm_kernel(a_ref, w_ref, o_ref, *, scale):
    acc = jnp.dot(a_ref[...], w_ref[...], preferred_element_type=F32)
    if scale != 1.0:
        acc = acc * scale
    o_ref[...] = acc.astype(o_ref.dtype)


def matmul(a, w, tm, tn, scale=1.0, out_dtype=BF16):
    m, k = a.shape
    n = w.shape[1]
    tn = _tile(n, tn)
    return pl.pallas_call(
        functools.partial(_mm_kernel, scale=scale),
        grid=(m // tm, n // tn),
        in_specs=[pl.BlockSpec((tm, k), lambda i, j: (i, 0)), pl.BlockSpec((k, tn), lambda i, j: (0, j))],
        out_specs=pl.BlockSpec((tm, tn), lambda i, j: (i, j)),
        out_shape=jax.ShapeDtypeStruct((m, n), out_dtype),
        compiler_params=_params("parallel", "arbitrary"),
        name="matmul",
    )(a, w)


def _mm_nt_kernel(wt_ref, a_ref, o_ref):
    acc = lax.dot_general(wt_ref[...], a_ref[...], (((1,), (1,)), ((), ())), preferred_element_type=F32)
    o_ref[...] = acc.astype(o_ref.dtype)


def matmul_transposed_out(a, wt, tm, tn):
    m, k = a.shape
    n = wt.shape[0]
    tn = _tile(n, tn)
    return pl.pallas_call(
        _mm_nt_kernel,
        grid=(m // tm, n // tn),
        in_specs=[pl.BlockSpec((tn, k), lambda i, j: (j, 0)), pl.BlockSpec((tm, k), lambda i, j: (i, 0))],
        out_specs=pl.BlockSpec((tn, tm), lambda i, j: (j, i)),
        out_shape=jax.ShapeDtypeStruct((n, m), BF16),
        compiler_params=_params("parallel", "arbitrary"),
        name="matmul_transposed_out",
    )(wt, a)


def _mm_rope_kernel(a_ref, w_ref, c_ref, sa_ref, sb_ref, o_ref, *, scale):
    acc = jnp.dot(a_ref[...], w_ref[...], preferred_element_type=F32)
    c = c_ref[...] * scale
    sa = sa_ref[...] * scale
    sb = sb_ref[...] * scale
    for g in range(acc.shape[1] // LANES):
        t = acc[:, g * LANES:(g + 1) * LANES]
        r = t * c + pltpu.roll(t, LANES - ROT_DIM // 2, 1) * sa + pltpu.roll(t, ROT_DIM // 2, 1) * sb
        o_ref[:, g * LANES:(g + 1) * LANES] = r.astype(o_ref.dtype)


def matmul_rope(a, w, tables, tm, tn, scale=1.0):
    m, k = a.shape
    n = w.shape[1]
    tn = _tile(n, tn)
    tab = pl.BlockSpec((tm, LANES), lambda i, j: (i, 0))
    return pl.pallas_call(
        functools.partial(_mm_rope_kernel, scale=scale),
        grid=(m // tm, n // tn),
        in_specs=[pl.BlockSpec((tm, k), lambda i, j: (i, 0)), pl.BlockSpec((k, tn), lambda i, j: (0, j)),
                  tab, tab, tab],
        out_specs=pl.BlockSpec((tm, tn), lambda i, j: (i, j)),
        out_shape=jax.ShapeDtypeStruct((m, n), BF16),
        compiler_params=_params("parallel", "arbitrary"),
        name="matmul_rope",
    )(a, w, *tables)


def _mm_glu_kernel(a_ref, wu_ref, wg_ref, o_ref):
    a = a_ref[...]
    u = jnp.dot(a, wu_ref[...], preferred_element_type=F32)
    g = jnp.dot(a, wg_ref[...], preferred_element_type=F32)
    o_ref[...] = u * jax.nn.sigmoid(g)


def matmul_glu(a, wu, wg, tm, tn):
    m, k = a.shape
    n = wu.shape[1]
    tn = _tile(n, tn)
    wspec = pl.BlockSpec((k, tn), lambda i, j: (0, j))
    return pl.pallas_call(
        _mm_glu_kernel,
        grid=(m // tm, n // tn),
        in_specs=[pl.BlockSpec((tm, k), lambda i, j: (i, 0)), wspec, wspec],
        out_specs=pl.BlockSpec((tm, tn), lambda i, j: (i, j)),
        out_shape=jax.ShapeDtypeStruct((m, n), F32),
        compiler_params=_params("parallel", "arbitrary"),
        name="matmul_glu",
    )(a, wu, wg)


def _mm2_kernel(a1_ref, a2_ref, w1_ref, w2_ref, o_ref):
    o_ref[...] = (jnp.dot(a1_ref[...], w1_ref[...], preferred_element_type=F32)
                  + jnp.dot(a2_ref[...], w2_ref[...], preferred_element_type=F32))


def matmul_concat(a1, a2, w1, w2, tm, tn):
    m, k1 = a1.shape
    k2 = a2.shape[1]
    n = w1.shape[1]
    tn = _tile(n, tn)
    return pl.pallas_call(
        _mm2_kernel,
        grid=(m // tm, n // tn),
        in_specs=[pl.BlockSpec((tm, k1), lambda i, j: (i, 0)), pl.BlockSpec((tm, k2), lambda i, j: (i, 0)),
                  pl.BlockSpec((k1, tn), lambda i, j: (0, j)), pl.BlockSpec((k2, tn), lambda i, j: (0, j))],
        out_specs=pl.BlockSpec((tm, tn), lambda i, j: (i, j)),
        out_shape=jax.ShapeDtypeStruct((m, n), F32),
        compiler_params=_params("parallel", "arbitrary"),
        name="matmul_concat",
    )(a1, a2, w1, w2)


def _mlp_kernel(h_ref, wu_ref, wd_ref, o_ref):
    @pl.when(pl.program_id(1) == 0)
    def _():
        o_ref[...] = jnp.zeros_like(o_ref)

    a = jnp.dot(h_ref[...], wu_ref[...], preferred_element_type=F32)
    a = jnp.maximum(a, 0.0)
    a = (a * a).astype(BF16)
    o_ref[...] += jnp.dot(a, wd_ref[...], preferred_element_type=F32)


def mlp(h, w_up, w_down, tm, tf):
    m, d = h.shape
    f = w_up.shape[1]
    return pl.pallas_call(
        _mlp_kernel,
        grid=(m // tm, f // tf),
        in_specs=[pl.BlockSpec((tm, d), lambda i, j: (i, 0), pipeline_mode=pl.Buffered(1)),
                  pl.BlockSpec((d, tf), lambda i, j: (0, j)), pl.BlockSpec((tf, d), lambda i, j: (j, 0))],
        out_specs=pl.BlockSpec((tm, d), lambda i, j: (i, 0), pipeline_mode=pl.Buffered(1)),
        out_shape=jax.ShapeDtypeStruct((m, d), F32),
        compiler_params=_params("parallel", "arbitrary"),
        name="mlp",
    )(h, w_up, w_down)


def _norm_residual_kernel(x_ref, y_ref, gp_ref, *rest, n_next):
    g_refs, x_out, h_outs = rest[:n_next], rest[n_next], rest[n_next + 1:]
    y = y_ref[...]
    x = x_ref[...] + (y * _rms_scale(y)) * gp_ref[...]
    x_out[...] = x
    if n_next:
        xn = x * _rms_scale(x)
        for g_ref, h_out in zip(g_refs, h_outs):
            h_out[...] = (xn * g_ref[...]).astype(h_out.dtype)


def norm_residual(x, y, g_post, g_next, tm):
    m, d = x.shape
    row = pl.BlockSpec((tm, d), lambda i: (i, 0))
    vec = pl.BlockSpec((1, d), lambda i: (0, 0))
    n_next = len(g_next)
    outs = pl.pallas_call(
        functools.partial(_norm_residual_kernel, n_next=n_next),
        grid=(m // tm,),
        in_specs=[row, row, vec] + [vec] * n_next,
        out_specs=[row] + [row] * n_next,
        out_shape=[jax.ShapeDtypeStruct((m, d), F32)] + [jax.ShapeDtypeStruct((m, d), BF16)] * n_next,
        compiler_params=_params("parallel"),
        name="norm_residual",
    )(x, y, g_post.reshape(1, d), *[g.reshape(1, d) for g in g_next])
    return outs[0], list(outs[1:])


def _conv_ln_kernel(halo_ref, cur_ref, w_ref, b_ref, g_ref, beta_ref, o_ref, buf_ref, acc_ref, *, width, rows, cw):
    ts, ch = cur_ref.shape
    first = pl.program_id(1) == 0
    buf_ref[0:CONV_HALO, :] = jnp.where(first, 0.0, halo_ref[...])
    buf_ref[CONV_HALO:, :] = cur_ref[...]
    lead = CONV_HALO - (width - 1)
    win = rows + CONV_HALO

    def chunk(ci, carry):
        cols = pl.ds(pl.multiple_of(ci * cw, cw), cw)
        for r0 in range(0, ts, rows):
            x = buf_ref[r0:r0 + win, cols]
            acc = b_ref[:, cols] + buf_ref[CONV_HALO + r0:CONV_HALO + r0 + rows, cols] * w_ref[width - 1:width, cols]
            for b in range(8):
                xb = x if b == 0 else pltpu.roll(x, win - b, 0)
                for a in range(CONV_HALO // 8):
                    k = 8 * a + b - lead
                    if 0 <= k < width - 1:
                        acc = acc + xb[8 * a:8 * a + rows] * w_ref[k:k + 1, cols]
            acc_ref[r0:r0 + rows, cols] = acc
        return carry

    lax.fori_loop(0, ch // cw, chunk, 0)

    def norm_rows(ri, carry):
        rs = pl.ds(pl.multiple_of(ri * LN_ROWS, LN_ROWS), LN_ROWS)
        c = acc_ref[rs, :]
        mu = jnp.mean(c, axis=-1, keepdims=True)
        cc = c - mu
        var = jnp.mean(cc * cc, axis=-1, keepdims=True)
        y = (cc * lax.rsqrt(var + EPS)) * g_ref[...] + beta_ref[...]
        o_ref[rs, :] = (y * jax.nn.sigmoid(y)).astype(o_ref.dtype)
        return carry

    lax.fori_loop(0, ts // LN_ROWS, norm_rows, 0, unroll=4)


def conv_ln_silu(c0, w, b, g, beta, batch, ts, rows=64, cw=128):
    t, ch = c0.shape
    seq = t // batch
    width = w.shape[0]
    nt = seq // ts
    hb = ts // CONV_HALO
    vec = pl.BlockSpec((1, ch), lambda bi, i: (0, 0))
    return pl.pallas_call(
        functools.partial(_conv_ln_kernel, width=width, rows=rows, cw=cw),
        grid=(batch, nt),
        in_specs=[pl.BlockSpec((CONV_HALO, ch), lambda bi, i: (jnp.maximum((bi * nt + i) * hb - 1, 0), 0)),
                  pl.BlockSpec((ts, ch), lambda bi, i: (bi * nt + i, 0)),
                  pl.BlockSpec((width, ch), lambda bi, i: (0, 0)), vec, vec, vec],
        out_specs=pl.BlockSpec((ts, ch), lambda bi, i: (bi * nt + i, 0)),
        out_shape=jax.ShapeDtypeStruct((t, ch), BF16),
        scratch_shapes=[pltpu.VMEM((ts + CONV_HALO, ch), F32), pltpu.VMEM((ts, ch), F32)],
        compiler_params=_params("parallel", "arbitrary"),
        name="conv_ln_silu",
    )(c0, c0, w, b.reshape(1, ch), g.reshape(1, ch), beta.reshape(1, ch))


def _mem_attn_kernel(q_ref, kv_ref, o_ref):
    width = q_ref.shape[1]
    hd = width // MEM_HEADS
    for h in range(MEM_HEADS):
        q = q_ref[:, h * hd:(h + 1) * hd]
        mk = kv_ref[:, h * hd:(h + 1) * hd]
        mv = kv_ref[:, width + h * hd:width + (h + 1) * hd]
        s = lax.dot_general(q, mk, (((1,), (1,)), ((), ())), preferred_element_type=F32)
        p = jnp.exp(s - jnp.max(s, axis=-1, keepdims=True))
        l = jnp.sum(p, axis=-1, keepdims=True)
        o = jnp.dot(p.astype(BF16), mv, preferred_element_type=F32)
        o_ref[:, h * hd:(h + 1) * hd] = (o / l).astype(o_ref.dtype)


def mem_attention(qm, mkv, batch, tm):
    t, width = qm.shape
    nt = t // batch // tm
    mt = mkv.shape[0] // batch
    return pl.pallas_call(
        _mem_attn_kernel,
        grid=(batch, nt),
        in_specs=[pl.BlockSpec((tm, width), lambda b, i: (b * nt + i, 0)),
                  pl.BlockSpec((mt, 2 * width), lambda b, i: (b, 0))],
        out_specs=pl.BlockSpec((tm, width), lambda b, i: (b * nt + i, 0)),
        out_shape=jax.ShapeDtypeStruct((t, width), BF16),
        compiler_params=_params("parallel", "arbitrary"),
        name="mem_attention",
    )(qm, mkv)


def _diff_attn_kernel(lam_ref, g_ref, q_ref, k_ref, vt_ref, o_ref, acc1_ref, acc2_ref, sa_ref, sb_ref, st_ref,
                      *, lambda_init):
    tq = q_ref.shape[0]
    d = DIFF_HEAD_DIM
    qi = pl.program_id(2)
    q = q_ref[...]
    qs = (q[:, :d], q[:, d:])
    accs = (acc1_ref, acc2_ref)
    for acc in accs:
        acc[...] = jnp.zeros_like(acc)
    st_ref[...] = jnp.where(lax.broadcasted_iota(jnp.int32, st_ref.shape, 0) % 2 == 0, -jnp.inf, 0.0)
    key = lax.broadcasted_iota(jnp.int32, (tq, tq), 0)
    qry = lax.broadcasted_iota(jnp.int32, (tq, tq), 1)
    causal = key <= qry
    nt = (((1,), (1,)), ((), ()))

    def scores(j, s_ref):
        k = k_ref[pl.ds(pl.multiple_of(j * tq, tq), tq), :]
        for c in range(2):
            s_ref[c] = lax.dot_general(k[:, c * d:(c + 1) * d], qs[c], nt, preferred_element_type=F32)

    def update(j, s_ref, masked):
        vt = vt_ref[:, pl.ds(pl.multiple_of(j * tq, tq), tq)]
        for c in range(2):
            m, l = st_ref[2 * c:2 * c + 1, :], st_ref[2 * c + 1:2 * c + 2, :]
            s = s_ref[c]
            if masked:
                s = jnp.where(causal, s, -jnp.inf)
            mn = jnp.maximum(m, jnp.max(s, axis=0, keepdims=True))
            p = jnp.exp2(s - mn)
            alpha = jnp.exp2(m - mn)
            st_ref[2 * c:2 * c + 1, :] = mn
            st_ref[2 * c + 1:2 * c + 2, :] = alpha * l + jnp.sum(p, axis=0, keepdims=True)
            accs[c][...] = accs[c][...] * alpha + jnp.dot(vt, p.astype(BF16), preferred_element_type=F32)

    scores(0, sa_ref)

    def pair(i, carry):
        j = 2 * i
        scores(j + 1, sb_ref)
        update(j, sa_ref, False)
        scores(j + 2, sa_ref)
        update(j + 1, sb_ref, False)
        return carry

    lax.fori_loop(0, qi // 2, pair, 0)

    @pl.when(qi % 2 == 1)
    def _():
        scores(qi, sb_ref)
        update(qi - 1, sa_ref, False)
        update(qi, sb_ref, True)

    @pl.when(qi % 2 == 0)
    def _():
        update(qi, sa_ref, True)

    l1, l2 = st_ref[1:2, :], st_ref[3:4, :]
    lm = lam_ref[...]
    lam = (jnp.exp(jnp.sum(lm[0:1] * lm[1:2], axis=-1, keepdims=True))
           - jnp.exp(jnp.sum(lm[2:3] * lm[3:4], axis=-1, keepdims=True)) + lambda_init)
    o = acc1_ref[...] / l1 - lam * (acc2_ref[...] / l2)
    inv = lax.rsqrt(jnp.mean(o * o, axis=0, keepdims=True) + EPS)
    o = (o * inv) * g_ref[...] * (1.0 - lambda_init)
    o_ref[...] = o.T.astype(o_ref.dtype)


def diff_attention(q, k, vt, lam_params, subln, batch, tq, lambda_init):
    t, width = q.shape
    seq = t // batch
    hw = 2 * DIFF_HEAD_DIM
    heads = width // hw
    nq = seq // tq
    return pl.pallas_call(
        functools.partial(_diff_attn_kernel, lambda_init=lambda_init),
        grid=(batch, heads, nq),
        in_specs=[pl.BlockSpec((4, DIFF_HEAD_DIM), lambda b, h, i: (0, 0)),
                  pl.BlockSpec((hw, 1), lambda b, h, i: (0, 0)),
                  pl.BlockSpec((tq, hw), lambda b, h, i: (b * nq + i, h)),
                  pl.BlockSpec((seq, hw), lambda b, h, i: (b, h)),
                  pl.BlockSpec((hw, seq), lambda b, h, i: (h, b))],
        out_specs=pl.BlockSpec((tq, hw), lambda b, h, i: (b * nq + i, h)),
        out_shape=jax.ShapeDtypeStruct((t, width), BF16),
        scratch_shapes=[pltpu.VMEM((hw, tq), F32), pltpu.VMEM((hw, tq), F32),
                        pltpu.VMEM((2, tq, tq), F32), pltpu.VMEM((2, tq, tq), F32), pltpu.VMEM((8, tq), F32)],
        compiler_params=_params("parallel", "parallel", "arbitrary"),
        name="diff_attention",
    )(lam_params, subln.reshape(hw, 1), q, k, vt)


def _tile(n, want):
    t = min(n, want)
    while n % t:
        t //= 2
    return t


def kernel(x, mem, positions, a_norm_pre, a_norm_post, a_mem_norm, a_w_in, a_conv_w, a_conv_b, a_ln_g, a_ln_b,
           a_w_mem_kv, a_w_out, kv_norm, w_kv, b_norm_pre, b_norm_post, b_mem_norm, b_w_in, b_lambda, b_subln,
           b_w_mem_kv, b_w_out, mlp_norm_pre, mlp_norm_post, w_up, w_down):
    batch, seq, d = x.shape
    t = batch * seq
    n_a = a_norm_pre.shape[0]
    depth = n_a + b_norm_pre.shape[0]
    conv_ch = a_conv_w.shape[2]
    mem_width = a_w_in.shape[2] - 2 * conv_ch
    qk_width = b_w_in.shape[2] - mem_width
    mem_scale = float((mem_width // MEM_HEADS) ** -0.5)

    tm = _tile(t, 1024)
    tn = 512
    te = _tile(t, 256)
    mem2 = mem.reshape(-1, d)
    tmem = _tile(mem2.shape[0], 512)

    xr = x.reshape(t, d)
    tables = rope_tables(positions, _tile(t, 1024))
    h = rmsnorm_bf16(xr, a_norm_pre[0] if n_a else b_norm_pre[0], te)
    k_sh = v_sh = None

    for layer in range(depth):
        last = layer == depth - 1
        if layer < n_a:
            i = layer
            w_in = a_w_in[i].astype(BF16)
            c0 = matmul_glu(h, w_in[:, :conv_ch], w_in[:, conv_ch:2 * conv_ch], tm, tn)
            qm = matmul(h, w_in[:, 2 * conv_ch:], tm, tn, scale=mem_scale)
            main = conv_ln_silu(c0, a_conv_w[i], a_conv_b[i], a_ln_g[i], a_ln_b[i], batch, _tile(seq, 128))
            mem_n = rmsnorm_bf16(mem2, a_mem_norm[i], _tile(mem2.shape[0], 256))
            mkv = matmul(mem_n, a_w_mem_kv[i].astype(BF16), tmem, tn)
            w_out = a_w_out[i].astype(BF16)
            g_post = a_norm_post[i]
        else:
            j = layer - n_a
            lambda_init = 0.8 - 0.6 * math.exp(-0.3 * layer)
            w_in = b_w_in[j].astype(BF16)
            q = matmul_rope(h, w_in[:, :qk_width], tables, tm, tn, scale=DIFF_HEAD_DIM ** -0.5 * LOG2E)
            qm = matmul(h, w_in[:, qk_width:], tm, tn, scale=mem_scale)
            main = diff_attention(q, k_sh, v_sh, b_lambda[j], b_subln[j], batch, _tile(seq, 512), lambda_init)
            mem_n = rmsnorm_bf16(mem2, b_mem_norm[j], _tile(mem2.shape[0], 256))
            mkv = matmul(mem_n, b_w_mem_kv[j].astype(BF16), tmem, tn)
            w_out = b_w_out[j].astype(BF16)
            g_post = b_norm_post[j]

        m_out = mem_attention(qm, mkv, batch, _tile(seq, 512))
        kmain = main.shape[1]
        y = matmul_concat(main, m_out, w_out[:kmain], w_out[kmain:], tm, tn)
        xr, (h,) = norm_residual(xr, y, g_post, [mlp_norm_pre[layer]], te)

        y = mlp(h, w_up[layer].astype(BF16), w_down[layer].astype(BF16), tm, 512)
        g_next = []
        if layer == n_a - 1:
            g_next.append(kv_norm)
        if not last:
            g_next.append(a_norm_pre[layer + 1] if layer + 1 < n_a else b_norm_pre[layer + 1 - n_a])
        xr, hs = norm_residual(xr, y, mlp_norm_post[layer], g_next, te)
        if layer == n_a - 1:
            w_kv_b = w_kv.astype(BF16)
            k_sh = matmul_rope(hs[0], w_kv_b[:, :qk_width], tables, tm, tn)
            v_sh = matmul_transposed_out(hs[0], w_kv[:, qk_width:].T.astype(BF16), tm, tn)
        if not last:
            h = hs[-1]

    return xr.reshape(batch, seq, d)
```

```python
import functools
import math

import jax
import jax.numpy as jnp
from jax import lax
from jax.experimental import pallas as pl
from jax.experimental.pallas import tpu as pltpu

F32 = jnp.float32
BF16 = jnp.bfloat16

EPS = 1e-6
ROPE_THETA = 500000.0
LANES = 128
BF16_SUBLANES = 16
MEM_HEADS = 4
DIFF_HEAD_DIM = 128
ROT_DIM = DIFF_HEAD_DIM // 4
LN_ROWS = 16
CONV_HALO = 32
VMEM_LIMIT = 56 * 1024 * 1024
LOG2E = 1.4426950408889634


def _params(*sem):
    return pltpu.CompilerParams(dimension_semantics=sem, vmem_limit_bytes=VMEM_LIMIT)


def _rms_scale(t):
    return lax.rsqrt(jnp.mean(t * t, axis=-1, keepdims=True) + EPS)


def _rmsnorm_kernel(x_ref, g_ref, o_ref):
    x = x_ref[...]
    o_ref[...] = ((x * _rms_scale(x)) * g_ref[...]).astype(o_ref.dtype)


def rmsnorm_bf16(x, g, tm):
    m, d = x.shape
    return pl.pallas_call(
        _rmsnorm_kernel,
        grid=(m // tm,),
        in_specs=[pl.BlockSpec((tm, d), lambda i: (i, 0)), pl.BlockSpec((1, d), lambda i: (0, 0))],
        out_specs=pl.BlockSpec((tm, d), lambda i: (i, 0)),
        out_shape=jax.ShapeDtypeStruct((m, d), BF16),
        compiler_params=_params("parallel"),
        name="rmsnorm",
    )(x, g.reshape(1, d))


def _rope_table_kernel(pos_ref, freq_ref, c_ref, sa_ref, sb_ref):
    ang = pos_ref[...] * freq_ref[...]
    lane = lax.broadcasted_iota(jnp.int32, ang.shape, 1)
    sin = jnp.sin(ang)
    c_ref[...] = jnp.where(lane < ROT_DIM, jnp.cos(ang), 1.0)
    sa_ref[...] = jnp.where(lane < ROT_DIM // 2, -sin, 0.0)
    sb_ref[...] = jnp.where((lane >= ROT_DIM // 2) & (lane < ROT_DIM), sin, 0.0)


def rope_tables(positions, tm):
    t = positions.size
    pos = jnp.broadcast_to(positions.reshape(t, 1).astype(F32), (t, LANES))
    inv_freq = jnp.power(jnp.float32(ROPE_THETA), -jnp.arange(0, ROT_DIM, 2, dtype=F32) / ROT_DIM)
    freq = jnp.concatenate([inv_freq, inv_freq, jnp.zeros((LANES - ROT_DIM,), F32)]).reshape(1, LANES)
    spec = pl.BlockSpec((tm, LANES), lambda i: (i, 0))
    shp = jax.ShapeDtypeStruct((t, LANES), F32)
    return pl.pallas_call(
        _rope_table_kernel,
        grid=(t // tm,),
        in_specs=[spec, pl.BlockSpec((1, LANES), lambda i: (0, 0))],
        out_specs=[spec, spec, spec],
        out_shape=[shp, shp, shp],
        compiler_params=_params("parallel"),
        name="rope_tables",
    )(pos, freq)


def _mm_kernel(a_ref, w_ref, o_ref, *, scale):
    acc = jnp.dot(a_ref[...], w_ref[...], preferred_element_type=F32)
    if scale != 1.0:
        acc = acc * scale
    o_ref[...] = acc.astype(o_ref.dtype)


def _col_window(w, tn, col0, ncols):
    n = w.shape[1] - col0 if ncols is None else ncols
    tn = _tile(math.gcd(n, col0) if col0 else n, tn)
    return n, tn, col0 // tn


def matmul(a, w, tm, tn, scale=1.0, out_dtype=BF16, col0=0, ncols=None):
    m, k = a.shape
    n, tn, j0 = _col_window(w, tn, col0, ncols)
    return pl.pallas_call(
        functools.partial(_mm_kernel, scale=scale),
        grid=(m // tm, n // tn),
        in_specs=[pl.BlockSpec((tm, k), lambda i, j: (i, 0)), pl.BlockSpec((k, tn), lambda i, j: (0, j + j0))],
        out_specs=pl.BlockSpec((tm, tn), lambda i, j: (i, j)),
        out_shape=jax.ShapeDtypeStruct((m, n), out_dtype),
        compiler_params=_params("parallel", "arbitrary"),
        name="matmul",
    )(a, w)


def _mm_nt_kernel(wt_ref, a_ref, o_ref):
    acc = lax.dot_general(wt_ref[...], a_ref[...], (((1,), (1,)), ((), ())), preferred_element_type=F32)
    o_ref[...] = acc.astype(o_ref.dtype)


def matmul_transposed_out(a, wt, tm, tn):
    m, k = a.shape
    n = wt.shape[0]
    tn = _tile(n, tn)
    return pl.pallas_call(
        _mm_nt_kernel,
        grid=(m // tm, n // tn),
        in_specs=[pl.BlockSpec((tn, k), lambda i, j: (j, 0)), pl.BlockSpec((tm, k), lambda i, j: (i, 0))],
        out_specs=pl.BlockSpec((tn, tm), lambda i, j: (j, i)),
        out_shape=jax.ShapeDtypeStruct((n, m), BF16),
        compiler_params=_params("parallel", "arbitrary"),
        name="matmul_transposed_out",
    )(wt, a)


def _mm_rope_kernel(a_ref, w_ref, c_ref, sa_ref, sb_ref, o_ref, *, scale):
    acc = jnp.dot(a_ref[...], w_ref[...], preferred_element_type=F32)
    c = c_ref[...] * scale
    sa = sa_ref[...] * scale
    sb = sb_ref[...] * scale
    for g in range(acc.shape[1] // LANES):
        t = acc[:, g * LANES:(g + 1) * LANES]
        r = t * c + pltpu.roll(t, LANES - ROT_DIM // 2, 1) * sa + pltpu.roll(t, ROT_DIM // 2, 1) * sb
        o_ref[:, g * LANES:(g + 1) * LANES] = r.astype(o_ref.dtype)


def matmul_rope(a, w, tables, tm, tn, scale=1.0, ncols=None):
    m, k = a.shape
    n, tn, _ = _col_window(w, tn, 0, ncols)
    tab = pl.BlockSpec((tm, LANES), lambda i, j: (i, 0))
    return pl.pallas_call(
        functools.partial(_mm_rope_kernel, scale=scale),
        grid=(m // tm, n // tn),
        in_specs=[pl.BlockSpec((tm, k), lambda i, j: (i, 0)), pl.BlockSpec((k, tn), lambda i, j: (0, j)),
                  tab, tab, tab],
        out_specs=pl.BlockSpec((tm, tn), lambda i, j: (i, j)),
        out_shape=jax.ShapeDtypeStruct((m, n), BF16),
        compiler_params=_params("parallel", "arbitrary"),
        name="matmul_rope",
    )(a, w, *tables)


def _mm_glu_kernel(a_ref, wu_ref, wg_ref, o_ref):
    a = a_ref[...]
    u = jnp.dot(a, wu_ref[...], preferred_element_type=F32)
    g = jnp.dot(a, wg_ref[...], preferred_element_type=F32)
    o_ref[...] = u * jax.nn.sigmoid(g)


def matmul_glu(a, w, n, tm, tn):
    m, k = a.shape
    tn = _tile(n, tn)
    nj = n // tn
    return pl.pallas_call(
        _mm_glu_kernel,
        grid=(m // tm, nj),
        in_specs=[pl.BlockSpec((tm, k), lambda i, j: (i, 0)), pl.BlockSpec((k, tn), lambda i, j: (0, j)),
                  pl.BlockSpec((k, tn), lambda i, j: (0, j + nj))],
        out_specs=pl.BlockSpec((tm, tn), lambda i, j: (i, j)),
        out_shape=jax.ShapeDtypeStruct((m, n), F32),
        compiler_params=_params("parallel", "arbitrary"),
        name="matmul_glu",
    )(a, w, w)


def _mm2_kernel(a1_ref, a2_ref, w1_ref, w2_ref, o_ref):
    acc = (jnp.dot(a1_ref[...], w1_ref[...], preferred_element_type=F32)
           + jnp.dot(a2_ref[...], w2_ref[...], preferred_element_type=F32))
    o_ref[...] = acc.astype(o_ref.dtype)


def matmul_concat(a1, a2, w, tm, tn):
    m, k1 = a1.shape
    k2 = a2.shape[1]
    n = w.shape[1]
    tn = _tile(n, tn)
    assert k1 % k2 == 0 and w.shape[0] == k1 + k2
    return pl.pallas_call(
        _mm2_kernel,
        grid=(m // tm, n // tn),
        in_specs=[pl.BlockSpec((tm, k1), lambda i, j: (i, 0)), pl.BlockSpec((tm, k2), lambda i, j: (i, 0)),
                  pl.BlockSpec((k1, tn), lambda i, j: (0, j)), pl.BlockSpec((k2, tn), lambda i, j: (k1 // k2, j))],
        out_specs=pl.BlockSpec((tm, tn), lambda i, j: (i, j)),
        out_shape=jax.ShapeDtypeStruct((m, n), BF16),
        compiler_params=_params("parallel", "arbitrary"),
        name="matmul_concat",
    )(a1, a2, w, w)


def _side_cast_specs(stacked, layer, steps, step_id):
    rows_total, cols = stacked.shape[1:]
    if rows_total % steps:
        return None
    rows = rows_total // steps
    hold = max(1, BF16_SUBLANES // rows)
    if (rows * hold) % BF16_SUBLANES or rows_total % (rows * hold):
        return None
    in_spec = pl.BlockSpec((None, rows * hold, cols), lambda *g: (layer, step_id(*g) // hold, 0))
    out_spec = pl.BlockSpec((rows * hold, cols), lambda *g: (step_id(*g) // hold, 0))
    return in_spec, out_spec, jax.ShapeDtypeStruct((rows_total, cols), BF16)


def _side_casts(arrays, steps, step_id):
    specs = [_side_cast_specs(a, layer, steps, step_id) for a, layer in arrays]
    if any(s is None for s in specs):
        return None
    return ([a for a, _ in arrays], [s[0] for s in specs], [s[1] for s in specs], [s[2] for s in specs])


def _run_side_casts(in_refs, out_refs):
    for i_ref, o_ref in zip(in_refs, out_refs):
        o_ref[...] = i_ref[...].astype(o_ref.dtype)


def _mlp_kernel(h_ref, wu_ref, wd_ref, *rest, n_side):
    side_in, o_ref, side_out = rest[:n_side], rest[n_side], rest[n_side + 1:]

    @pl.when(pl.program_id(1) == 0)
    def _():
        o_ref[...] = jnp.zeros_like(o_ref)

    a = jnp.dot(h_ref[...], wu_ref[...], preferred_element_type=F32)
    a = jnp.maximum(a, 0.0)
    a = (a * a).astype(BF16)
    o_ref[...] += jnp.dot(a, wd_ref[...], preferred_element_type=F32)
    _run_side_casts(side_in, side_out)


def mlp(h, w_up, w_down, tm, tf, cast_arrays=()):
    m, d = h.shape
    f = w_up.shape[1]
    nj = f // tf
    side = _side_casts(cast_arrays, (m // tm) * nj, lambda i, j: i * nj + j) if cast_arrays else None
    ops, s_in, s_out, s_shape = side if side else ([], [], [], [])
    outs = pl.pallas_call(
        functools.partial(_mlp_kernel, n_side=len(ops)),
        grid=(m // tm, nj),
        in_specs=[pl.BlockSpec((tm, d), lambda i, j: (i, 0), pipeline_mode=pl.Buffered(1)),
                  pl.BlockSpec((d, tf), lambda i, j: (0, j)), pl.BlockSpec((tf, d), lambda i, j: (j, 0))] + s_in,
        out_specs=[pl.BlockSpec((tm, d), lambda i, j: (i, 0), pipeline_mode=pl.Buffered(1))] + s_out,
        out_shape=[jax.ShapeDtypeStruct((m, d), F32)] + s_shape,
        compiler_params=_params("arbitrary", "arbitrary"),
        name="mlp",
    )(h, w_up, w_down, *ops)
    casts = list(outs[1:]) if side else [a[layer].astype(BF16) for a, layer in cast_arrays]
    return outs[0], casts


def _norm_residual_kernel(x_ref, y_ref, gp_ref, *rest, n_next):
    g_refs, x_out, h_outs = rest[:n_next], rest[n_next], rest[n_next + 1:]
    y = y_ref[...].astype(F32)
    x = x_ref[...] + (y * _rms_scale(y)) * gp_ref[...]
    x_out[...] = x
    if n_next:
        xn = x * _rms_scale(x)
        for g_ref, h_out in zip(g_refs, h_outs):
            h_out[...] = (xn * g_ref[...]).astype(h_out.dtype)


def norm_residual(x, y, g_post, g_next, tm):
    m, d = x.shape
    row = pl.BlockSpec((tm, d), lambda i: (i, 0))
    vec = pl.BlockSpec((1, d), lambda i: (0, 0))
    n_next = len(g_next)
    outs = pl.pallas_call(
        functools.partial(_norm_residual_kernel, n_next=n_next),
        grid=(m // tm,),
        in_specs=[row, row, vec] + [vec] * n_next,
        out_specs=[row] + [row] * n_next,
        out_shape=[jax.ShapeDtypeStruct((m, d), F32)] + [jax.ShapeDtypeStruct((m, d), BF16)] * n_next,
        compiler_params=_params("parallel"),
        name="norm_residual",
    )(x, y, g_post.reshape(1, d), *[g.reshape(1, d) for g in g_next])
    return outs[0], list(outs[1:])


def _conv_ln_kernel(halo_ref, cur_ref, w_ref, b_ref, g_ref, beta_ref, *rest, width, rows, cw, n_side):
    side_in, o_ref, side_out = rest[:n_side], rest[n_side], rest[n_side + 1:2 * n_side + 1]
    buf_ref, acc_ref = rest[2 * n_side + 1:]
    _run_side_casts(side_in, side_out)
    ts, ch = cur_ref.shape
    first = pl.program_id(1) == 0
    buf_ref[0:CONV_HALO, :] = jnp.where(first, 0.0, halo_ref[...])
    buf_ref[CONV_HALO:, :] = cur_ref[...]
    lead = CONV_HALO - (width - 1)
    win = rows + CONV_HALO

    def chunk(ci, carry):
        cols = pl.ds(pl.multiple_of(ci * cw, cw), cw)
        for r0 in range(0, ts, rows):
            x = buf_ref[r0:r0 + win, cols]
            acc = b_ref[:, cols] + buf_ref[CONV_HALO + r0:CONV_HALO + r0 + rows, cols] * w_ref[width - 1:width, cols]
            for b in range(8):
                xb = x if b == 0 else pltpu.roll(x, win - b, 0)
                for a in range(CONV_HALO // 8):
                    k = 8 * a + b - lead
                    if 0 <= k < width - 1:
                        acc = acc + xb[8 * a:8 * a + rows] * w_ref[k:k + 1, cols]
            acc_ref[r0:r0 + rows, cols] = acc
        return carry

    lax.fori_loop(0, ch // cw, chunk, 0)

    def norm_rows(ri, carry):
        rs = pl.ds(pl.multiple_of(ri * LN_ROWS, LN_ROWS), LN_ROWS)
        c = acc_ref[rs, :]
        mu = jnp.mean(c, axis=-1, keepdims=True)
        cc = c - mu
        var = jnp.mean(cc * cc, axis=-1, keepdims=True)
        y = (cc * lax.rsqrt(var + EPS)) * g_ref[...] + beta_ref[...]
        o_ref[rs, :] = (y * jax.nn.sigmoid(y)).astype(o_ref.dtype)
        return carry

    lax.fori_loop(0, ts // LN_ROWS, norm_rows, 0, unroll=4)


def conv_ln_silu(c0, w, b, g, beta, batch, ts, rows=64, cw=128, cast_arrays=()):
    t, ch = c0.shape
    seq = t // batch
    width = w.shape[0]
    nt = seq // ts
    hb = ts // CONV_HALO
    vec = pl.BlockSpec((1, ch), lambda bi, i: (0, 0))
    side = _side_casts(cast_arrays, batch * nt, lambda bi, i: bi * nt + i) if cast_arrays else None
    ops, s_in, s_out, s_shape = side if side else ([], [], [], [])
    outs = pl.pallas_call(
        functools.partial(_conv_ln_kernel, width=width, rows=rows, cw=cw, n_side=len(ops)),
        grid=(batch, nt),
        in_specs=[pl.BlockSpec((CONV_HALO, ch), lambda bi, i: (jnp.maximum((bi * nt + i) * hb - 1, 0), 0)),
                  pl.BlockSpec((ts, ch), lambda bi, i: (bi * nt + i, 0)),
                  pl.BlockSpec((width, ch), lambda bi, i: (0, 0)), vec, vec, vec] + s_in,
        out_specs=[pl.BlockSpec((ts, ch), lambda bi, i: (bi * nt + i, 0))] + s_out,
        out_shape=[jax.ShapeDtypeStruct((t, ch), BF16)] + s_shape,
        scratch_shapes=[pltpu.VMEM((ts + CONV_HALO, ch), F32), pltpu.VMEM((ts, ch), F32)],
        compiler_params=_params("arbitrary", "arbitrary"),
        name="conv_ln_silu",
    )(c0, c0, w, b.reshape(1, ch), g.reshape(1, ch), beta.reshape(1, ch), *ops)
    casts = list(outs[1:]) if side else [a[layer].astype(BF16) for a, layer in cast_arrays]
    return outs[0], casts


def _mem_attn_kernel(q_ref, kv_ref, o_ref):
    width = q_ref.shape[1]
    hd = width // MEM_HEADS
    for h in range(MEM_HEADS):
        q = q_ref[:, h * hd:(h + 1) * hd]
        mk = kv_ref[:, h * hd:(h + 1) * hd]
        mv = kv_ref[:, width + h * hd:width + (h + 1) * hd]
        s = lax.dot_general(q, mk, (((1,), (1,)), ((), ())), preferred_element_type=F32)
        p = jnp.exp(s - jnp.max(s, axis=-1, keepdims=True))
        l = jnp.sum(p, axis=-1, keepdims=True)
        o = jnp.dot(p.astype(BF16), mv, preferred_element_type=F32)
        o_ref[:, h * hd:(h + 1) * hd] = (o / l).astype(o_ref.dtype)


def mem_attention(qm, mkv, batch, tm):
    t, width = qm.shape
    nt = t // batch // tm
    mt = mkv.shape[0] // batch
    return pl.pallas_call(
        _mem_attn_kernel,
        grid=(batch, nt),
        in_specs=[pl.BlockSpec((tm, width), lambda b, i: (b * nt + i, 0)),
                  pl.BlockSpec((mt, 2 * width), lambda b, i: (b, 0))],
        out_specs=pl.BlockSpec((tm, width), lambda b, i: (b * nt + i, 0)),
        out_shape=jax.ShapeDtypeStruct((t, width), BF16),
        compiler_params=_params("parallel", "arbitrary"),
        name="mem_attention",
    )(qm, mkv)


def _diff_attn_kernel(lam_ref, g_ref, q_ref, k_ref, vt_ref, o_ref, acc1_ref, acc2_ref, sa_ref, sb_ref, st_ref,
                      *, lambda_init):
    tq = q_ref.shape[0]
    d = DIFF_HEAD_DIM
    qi = pl.program_id(2)
    q = q_ref[...]
    qs = (q[:, :d], q[:, d:])
    accs = (acc1_ref, acc2_ref)
    for acc in accs:
        acc[...] = jnp.zeros_like(acc)
    st_ref[...] = jnp.where(lax.broadcasted_iota(jnp.int32, st_ref.shape, 0) % 2 == 0, -jnp.inf, 0.0)
    key = lax.broadcasted_iota(jnp.int32, (tq, tq), 0)
    qry = lax.broadcasted_iota(jnp.int32, (tq, tq), 1)
    causal = key <= qry
    nt = (((1,), (1,)), ((), ()))

    def scores(j, s_ref):
        k = k_ref[pl.ds(pl.multiple_of(j * tq, tq), tq), :]
        for c in range(2):
            s_ref[c] = lax.dot_general(k[:, c * d:(c + 1) * d], qs[c], nt, preferred_element_type=F32)

    def update(j, s_ref, masked):
        vt = vt_ref[:, pl.ds(pl.multiple_of(j * tq, tq), tq)]
        for c in range(2):
            m, l = st_ref[2 * c:2 * c + 1, :], st_ref[2 * c + 1:2 * c + 2, :]
            s = s_ref[c]
            if masked:
                s = jnp.where(causal, s, -jnp.inf)
            mn = jnp.maximum(m, jnp.max(s, axis=0, keepdims=True))
            p = jnp.exp2(s - mn)
            alpha = jnp.exp2(m - mn)
            st_ref[2 * c:2 * c + 1, :] = mn
            st_ref[2 * c + 1:2 * c + 2, :] = alpha * l + jnp.sum(p, axis=0, keepdims=True)
            accs[c][...] = accs[c][...] * alpha + jnp.dot(vt, p.astype(BF16), preferred_element_type=F32)

    scores(0, sa_ref)

    def pair(i, carry):
        j = 2 * i
        scores(j + 1, sb_ref)
        update(j, sa_ref, False)
        scores(j + 2, sa_ref)
        update(j + 1, sb_ref, False)
        return carry

    lax.fori_loop(0, qi // 2, pair, 0)

    @pl.when(qi % 2 == 1)
    def _():
        scores(qi, sb_ref)
        update(qi - 1, sa_ref, False)
        update(qi, sb_ref, True)

    @pl.when(qi % 2 == 0)
    def _():
        update(qi, sa_ref, True)

    l1, l2 = st_ref[1:2, :], st_ref[3:4, :]
    lm = lam_ref[...]
    lam = (jnp.exp(jnp.sum(lm[0:1] * lm[1:2], axis=-1, keepdims=True))
           - jnp.exp(jnp.sum(lm[2:3] * lm[3:4], axis=-1, keepdims=True)) + lambda_init)
    o = acc1_ref[...] / l1 - lam * (acc2_ref[...] / l2)
    inv = lax.rsqrt(jnp.mean(o * o, axis=0, keepdims=True) + EPS)
    o = (o * inv) * g_ref[...] * (1.0 - lambda_init)
    o_ref[...] = o.T.astype(o_ref.dtype)


def diff_attention(q, k, vt, lam_params, subln, batch, tq, lambda_init):
    t, width = q.shape
    seq = t // batch
    hw = 2 * DIFF_HEAD_DIM
    heads = width // hw
    nq = seq // tq
    return pl.pallas_call(
        functools.partial(_diff_attn_kernel, lambda_init=lambda_init),
        grid=(batch, heads, nq),
        in_specs=[pl.BlockSpec((4, DIFF_HEAD_DIM), lambda b, h, i: (0, 0)),
                  pl.BlockSpec((hw, 1), lambda b, h, i: (0, 0)),
                  pl.BlockSpec((tq, hw), lambda b, h, i: (b * nq + i, h)),
                  pl.BlockSpec((seq, hw), lambda b, h, i: (b, h)),
                  pl.BlockSpec((hw, seq), lambda b, h, i: (h, b))],
        out_specs=pl.BlockSpec((tq, hw), lambda b, h, i: (b * nq + i, h)),
        out_shape=jax.ShapeDtypeStruct((t, width), BF16),
        scratch_shapes=[pltpu.VMEM((hw, tq), F32), pltpu.VMEM((hw, tq), F32),
                        pltpu.VMEM((2, tq, tq), F32), pltpu.VMEM((2, tq, tq), F32), pltpu.VMEM((8, tq), F32)],
        compiler_params=_params("parallel", "parallel", "arbitrary"),
        name="diff_attention",
    )(lam_params, subln.reshape(hw, 1), q, k, vt)


def _tile(n, want):
    t = min(n, want)
    while n % t:
        t //= 2
    return t


def kernel(x, mem, positions, a_norm_pre, a_norm_post, a_mem_norm, a_w_in, a_conv_w, a_conv_b, a_ln_g, a_ln_b,
           a_w_mem_kv, a_w_out, kv_norm, w_kv, b_norm_pre, b_norm_post, b_mem_norm, b_w_in, b_lambda, b_subln,
           b_w_mem_kv, b_w_out, mlp_norm_pre, mlp_norm_post, w_up, w_down):
    batch, seq, d = x.shape
    t = batch * seq
    n_a = a_norm_pre.shape[0]
    depth = n_a + b_norm_pre.shape[0]
    conv_ch = a_conv_w.shape[2]
    mem_width = a_w_in.shape[2] - 2 * conv_ch
    qk_width = b_w_in.shape[2] - mem_width
    mem_scale = float((mem_width // MEM_HEADS) ** -0.5)

    tm = _tile(t, 1024)
    tn = 1024
    tn_glu = 512
    te = _tile(t, 256)
    mem2 = mem.reshape(-1, d)
    tmem = _tile(mem2.shape[0], 512)

    xr = x.reshape(t, d)
    tables = rope_tables(positions, _tile(t, 1024))
    h = rmsnorm_bf16(xr, a_norm_pre[0] if n_a else b_norm_pre[0], te)
    k_sh = v_sh = None
    mlp_w = None

    for layer in range(depth):
        last = layer == depth - 1
        if layer < n_a:
            i = layer
            w_in = a_w_in[i].astype(BF16)
            c0 = matmul_glu(h, w_in, conv_ch, tm, tn_glu)
            qm = matmul(h, w_in, tm, tn, scale=mem_scale, col0=2 * conv_ch)
            cast = () if mlp_w else ((w_up, layer), (w_down, layer))
            main, casts = conv_ln_silu(c0, a_conv_w[i], a_conv_b[i], a_ln_g[i], a_ln_b[i], batch, _tile(seq, 128),
                                       cast_arrays=cast)
            mlp_w = mlp_w or casts
            mem_n = rmsnorm_bf16(mem2, a_mem_norm[i], _tile(mem2.shape[0], 256))
            mkv = matmul(mem_n, a_w_mem_kv[i].astype(BF16), tmem, tn)
            w_out = a_w_out[i].astype(BF16)
            g_post = a_norm_post[i]
        else:
            j = layer - n_a
            lambda_init = 0.8 - 0.6 * math.exp(-0.3 * layer)
            w_in = b_w_in[j].astype(BF16)
            q = matmul_rope(h, w_in, tables, tm, tn, scale=DIFF_HEAD_DIM ** -0.5 * LOG2E, ncols=qk_width)
            qm = matmul(h, w_in, tm, tn, scale=mem_scale, col0=qk_width)
            main = diff_attention(q, k_sh, v_sh, b_lambda[j], b_subln[j], batch, _tile(seq, 512), lambda_init)
            mem_n = rmsnorm_bf16(mem2, b_mem_norm[j], _tile(mem2.shape[0], 256))
            mkv = matmul(mem_n, b_w_mem_kv[j].astype(BF16), tmem, tn)
            w_out = b_w_out[j].astype(BF16)
            g_post = b_norm_post[j]

        m_out = mem_attention(qm, mkv, batch, _tile(seq, 512))
        y = matmul_concat(main, m_out, w_out, tm, tn)
        xr, (h,) = norm_residual(xr, y, g_post, [mlp_norm_pre[layer]], te)

        mlp_w = mlp_w or [w_up[layer].astype(BF16), w_down[layer].astype(BF16)]
        nxt = () if last else ((w_up, layer + 1), (w_down, layer + 1))
        y, casts = mlp(h, mlp_w[0], mlp_w[1], tm, 512, cast_arrays=nxt)
        mlp_w = casts or None
        g_next = []
        if layer == n_a - 1:
            g_next.append(kv_norm)
        if not last:
            g_next.append(a_norm_pre[layer + 1] if layer + 1 < n_a else b_norm_pre[layer + 1 - n_a])
        xr, hs = norm_residual(xr, y, mlp_norm_post[layer], g_next, te)
        if layer == n_a - 1:
            k_sh = matmul_rope(hs[0], w_kv[:, :qk_width].astype(BF16), tables, tm, tn)
            v_sh = matmul_transposed_out(hs[0], w_kv[:, qk_width:].T.astype(BF16), tm, 512)
        if not last:
            h = hs[-1]

    return xr.reshape(batch, seq, d)
```

```python
import functools
import math

import jax
import jax.numpy as jnp
from jax import lax
from jax.experimental import pallas as pl
from jax.experimental.pallas import tpu as pltpu

F32 = jnp.float32
BF16 = jnp.bfloat16

EPS = 1e-6
ROPE_THETA = 500000.0
LANES = 128
BF16_SUBLANES = 16
MEM_HEADS = 4
DIFF_HEAD_DIM = 128
ROT_DIM = DIFF_HEAD_DIM // 4
LN_ROWS = 16
CONV_HALO = 32
VMEM_LIMIT = 56 * 1024 * 1024
MLP_VMEM_LIMIT = 60 * 1024 * 1024
LOG2E = 1.4426950408889634


def _params(*sem):
    return pltpu.CompilerParams(dimension_semantics=sem, vmem_limit_bytes=VMEM_LIMIT)


def _rms_scale(t):
    return lax.rsqrt(jnp.mean(t * t, axis=-1, keepdims=True) + EPS)


def _rmsnorm_kernel(x_ref, g_ref, o_ref):
    x = x_ref[...]
    o_ref[...] = ((x * _rms_scale(x)) * g_ref[...]).astype(o_ref.dtype)


def rmsnorm_bf16(x, g, tm):
    m, d = x.shape
    return pl.pallas_call(
        _rmsnorm_kernel,
        grid=(m // tm,),
        in_specs=[pl.BlockSpec((tm, d), lambda i: (i, 0)), pl.BlockSpec((1, d), lambda i: (0, 0))],
        out_specs=pl.BlockSpec((tm, d), lambda i: (i, 0)),
        out_shape=jax.ShapeDtypeStruct((m, d), BF16),
        compiler_params=_params("parallel"),
        name="rmsnorm",
    )(x, g.reshape(1, d))


def _rope_table_kernel(pos_ref, freq_ref, c_ref, sa_ref, sb_ref):
    ang = pos_ref[...] * freq_ref[...]
    lane = lax.broadcasted_iota(jnp.int32, ang.shape, 1)
    sin = jnp.sin(ang)
    c_ref[...] = jnp.where(lane < ROT_DIM, jnp.cos(ang), 1.0)
    sa_ref[...] = jnp.where(lane < ROT_DIM // 2, -sin, 0.0)
    sb_ref[...] = jnp.where((lane >= ROT_DIM // 2) & (lane < ROT_DIM), sin, 0.0)


def rope_tables(positions, tm):
    t = positions.size
    pos = jnp.broadcast_to(positions.reshape(t, 1).astype(F32), (t, LANES))
    inv_freq = jnp.power(jnp.float32(ROPE_THETA), -jnp.arange(0, ROT_DIM, 2, dtype=F32) / ROT_DIM)
    freq = jnp.concatenate([inv_freq, inv_freq, jnp.zeros((LANES - ROT_DIM,), F32)]).reshape(1, LANES)
    spec = pl.BlockSpec((tm, LANES), lambda i: (i, 0))
    shp = jax.ShapeDtypeStruct((t, LANES), F32)
    return pl.pallas_call(
        _rope_table_kernel,
        grid=(t // tm,),
        in_specs=[spec, pl.BlockSpec((1, LANES), lambda i: (0, 0))],
        out_specs=[spec, spec, spec],
        out_shape=[shp, shp, shp],
        compiler_params=_params("parallel"),
        name="rope_tables",
    )(pos, freq)


def _mm_kernel(a_ref, w_ref, o_ref, *, scale):
    acc = jnp.dot(a_ref[...], w_ref[...], preferred_element_type=F32)
    if scale != 1.0:
        acc = acc * scale
    o_ref[...] = acc.astype(o_ref.dtype)


def _col_window(w, tn, col0, ncols):
    n = w.shape[1] - col0 if ncols is None else ncols
    tn = _tile(math.gcd(n, col0) if col0 else n, tn)
    return n, tn, col0 // tn


def matmul(a, w, tm, tn, scale=1.0, out_dtype=BF16, col0=0, ncols=None):
    m, k = a.shape
    n, tn, j0 = _col_window(w, tn, col0, ncols)
    return pl.pallas_call(
        functools.partial(_mm_kernel, scale=scale),
        grid=(m // tm, n // tn),
        in_specs=[pl.BlockSpec((tm, k), lambda i, j: (i, 0)), pl.BlockSpec((k, tn), lambda i, j: (0, j + j0))],
        out_specs=pl.BlockSpec((tm, tn), lambda i, j: (i, j)),
        out_shape=jax.ShapeDtypeStruct((m, n), out_dtype),
        compiler_params=_params("parallel", "arbitrary"),
        name="matmul",
    )(a, w)


def _mm_nt_kernel(wt_ref, a_ref, o_ref):
    acc = lax.dot_general(wt_ref[...], a_ref[...], (((1,), (1,)), ((), ())), preferred_element_type=F32)
    o_ref[...] = acc.astype(o_ref.dtype)


def matmul_transposed_out(a, wt, tm, tn):
    m, k = a.shape
    n = wt.shape[0]
    tn = _tile(n, tn)
    return pl.pallas_call(
        _mm_nt_kernel,
        grid=(m // tm, n // tn),
        in_specs=[pl.BlockSpec((tn, k), lambda i, j: (j, 0)), pl.BlockSpec((tm, k), lambda i, j: (i, 0))],
        out_specs=pl.BlockSpec((tn, tm), lambda i, j: (j, i)),
        out_shape=jax.ShapeDtypeStruct((n, m), BF16),
        compiler_params=_params("parallel", "arbitrary"),
        name="matmul_transposed_out",
    )(wt, a)


def _mm_rope_kernel(a_ref, w_ref, c_ref, sa_ref, sb_ref, o_ref, *, scale):
    acc = jnp.dot(a_ref[...], w_ref[...], preferred_element_type=F32)
    c = c_ref[...] * scale
    sa = sa_ref[...] * scale
    sb = sb_ref[...] * scale
    for g in range(acc.shape[1] // LANES):
        t = acc[:, g * LANES:(g + 1) * LANES]
        r = t * c + pltpu.roll(t, LANES - ROT_DIM // 2, 1) * sa + pltpu.roll(t, ROT_DIM // 2, 1) * sb
        o_ref[:, g * LANES:(g + 1) * LANES] = r.astype(o_ref.dtype)


def matmul_rope(a, w, tables, tm, tn, scale=1.0, ncols=None):
    m, k = a.shape
    n, tn, _ = _col_window(w, tn, 0, ncols)
    tab = pl.BlockSpec((tm, LANES), lambda i, j: (i, 0))
    return pl.pallas_call(
        functools.partial(_mm_rope_kernel, scale=scale),
        grid=(m // tm, n // tn),
        in_specs=[pl.BlockSpec((tm, k), lambda i, j: (i, 0)), pl.BlockSpec((k, tn), lambda i, j: (0, j)),
                  tab, tab, tab],
        out_specs=pl.BlockSpec((tm, tn), lambda i, j: (i, j)),
        out_shape=jax.ShapeDtypeStruct((m, n), BF16),
        compiler_params=_params("parallel", "arbitrary"),
        name="matmul_rope",
    )(a, w, *tables)


def _mm_glu_kernel(a_ref, wu_ref, wg_ref, o_ref):
    a = a_ref[...]
    u = jnp.dot(a, wu_ref[...], preferred_element_type=F32)
    g = jnp.dot(a, wg_ref[...], preferred_element_type=F32)
    o_ref[...] = u * jax.nn.sigmoid(g)


def matmul_glu(a, w, n, tm, tn):
    m, k = a.shape
    tn = _tile(n, tn)
    nj = n // tn
    return pl.pallas_call(
        _mm_glu_kernel,
        grid=(m // tm, nj),
        in_specs=[pl.BlockSpec((tm, k), lambda i, j: (i, 0)), pl.BlockSpec((k, tn), lambda i, j: (0, j)),
                  pl.BlockSpec((k, tn), lambda i, j: (0, j + nj))],
        out_specs=pl.BlockSpec((tm, tn), lambda i, j: (i, j)),
        out_shape=jax.ShapeDtypeStruct((m, n), F32),
        compiler_params=_params("parallel", "arbitrary"),
        name="matmul_glu",
    )(a, w, w)


def _mm2_kernel(a1_ref, a2_ref, w1_ref, w2_ref, o_ref):
    acc = (jnp.dot(a1_ref[...], w1_ref[...], preferred_element_type=F32)
           + jnp.dot(a2_ref[...], w2_ref[...], preferred_element_type=F32))
    o_ref[...] = acc.astype(o_ref.dtype)


def matmul_concat(a1, a2, w, tm, tn):
    m, k1 = a1.shape
    k2 = a2.shape[1]
    n = w.shape[1]
    tn = _tile(n, tn)
    assert k1 % k2 == 0 and w.shape[0] == k1 + k2
    return pl.pallas_call(
        _mm2_kernel,
        grid=(m // tm, n // tn),
        in_specs=[pl.BlockSpec((tm, k1), lambda i, j: (i, 0)), pl.BlockSpec((tm, k2), lambda i, j: (i, 0)),
                  pl.BlockSpec((k1, tn), lambda i, j: (0, j)), pl.BlockSpec((k2, tn), lambda i, j: (k1 // k2, j))],
        out_specs=pl.BlockSpec((tm, tn), lambda i, j: (i, j)),
        out_shape=jax.ShapeDtypeStruct((m, n), BF16),
        compiler_params=_params("parallel", "arbitrary"),
        name="matmul_concat",
    )(a1, a2, w, w)


def _plain_cast(stacked, layer, ncols=None):
    return stacked[layer, :, :ncols].astype(BF16)


def _side_cast_specs(stacked, layer, steps, step_id, ncols=None):
    rows_total, cols = stacked.shape[1:]
    cols = ncols or cols
    if rows_total % steps or stacked.shape[2] % cols:
        return None
    rows = rows_total // steps
    hold = max(1, BF16_SUBLANES // rows)
    if (rows * hold) % BF16_SUBLANES or rows_total % (rows * hold):
        return None
    in_spec = pl.BlockSpec((None, rows * hold, cols), lambda *g: (layer, step_id(*g) // hold, 0))
    out_spec = pl.BlockSpec((rows * hold, cols), lambda *g: (step_id(*g) // hold, 0))
    return in_spec, out_spec, jax.ShapeDtypeStruct((rows_total, cols), BF16)


def _side_casts(arrays, steps, step_id):
    specs = [_side_cast_specs(e[0], e[1], steps, step_id, *e[2:]) for e in arrays]
    if any(s is None for s in specs):
        return None
    return ([e[0] for e in arrays], [s[0] for s in specs], [s[1] for s in specs], [s[2] for s in specs])


def _run_side_casts(in_refs, out_refs):
    for i_ref, o_ref in zip(in_refs, out_refs):
        o_ref[...] = i_ref[...].astype(o_ref.dtype)


def _mlp_kernel(h_ref, wu_ref, wd_ref, *rest, n_side):
    side_in, o_ref, side_out, acc_ref = rest[:n_side], rest[n_side], rest[n_side + 1:-1], rest[-1]

    @pl.when(pl.program_id(1) == 0)
    def _():
        acc_ref[...] = jnp.zeros_like(acc_ref)

    a = jnp.dot(h_ref[...], wu_ref[...], preferred_element_type=F32)
    a = jnp.maximum(a, 0.0)
    a = (a * a).astype(BF16)
    acc_ref[...] += jnp.dot(a, wd_ref[...], preferred_element_type=F32)
    _run_side_casts(side_in, side_out)

    @pl.when(pl.program_id(1) == pl.num_programs(1) - 1)
    def _():
        o_ref[...] = acc_ref[...].astype(o_ref.dtype)


def mlp(h, w_up, w_down, tm, tf, cast_arrays=()):
    m, d = h.shape
    f = w_up.shape[1]
    nj = f // tf
    side = _side_casts(cast_arrays, (m // tm) * nj, lambda i, j: i * nj + j) if cast_arrays else None
    ops, s_in, s_out, s_shape = side if side else ([], [], [], [])
    outs = pl.pallas_call(
        functools.partial(_mlp_kernel, n_side=len(ops)),
        grid=(m // tm, nj),
        in_specs=[pl.BlockSpec((tm, d), lambda i, j: (i, 0), pipeline_mode=pl.Buffered(1)),
                  pl.BlockSpec((d, tf), lambda i, j: (0, j)), pl.BlockSpec((tf, d), lambda i, j: (j, 0))] + s_in,
        out_specs=[pl.BlockSpec((tm, d), lambda i, j: (i, 0), pipeline_mode=pl.Buffered(1))] + s_out,
        out_shape=[jax.ShapeDtypeStruct((m, d), BF16)] + s_shape,
        scratch_shapes=[pltpu.VMEM((tm, d), F32)],
        compiler_params=pltpu.CompilerParams(dimension_semantics=("arbitrary", "arbitrary"),
                                             vmem_limit_bytes=MLP_VMEM_LIMIT),
        name="mlp",
    )(h, w_up, w_down, *ops)
    casts = list(outs[1:]) if side else [_plain_cast(*entry) for entry in cast_arrays]
    return outs[0], casts


def _norm_residual_kernel(x_ref, y_ref, gp_ref, *rest, n_next):
    g_refs, x_out, h_outs = rest[:n_next], rest[n_next], rest[n_next + 1:]
    y = y_ref[...].astype(F32)
    x = x_ref[...] + (y * _rms_scale(y)) * gp_ref[...]
    x_out[...] = x
    if n_next:
        xn = x * _rms_scale(x)
        for g_ref, h_out in zip(g_refs, h_outs):
            h_out[...] = (xn * g_ref[...]).astype(h_out.dtype)


def norm_residual(x, y, g_post, g_next, tm):
    m, d = x.shape
    row = pl.BlockSpec((tm, d), lambda i: (i, 0))
    vec = pl.BlockSpec((1, d), lambda i: (0, 0))
    n_next = len(g_next)
    outs = pl.pallas_call(
        functools.partial(_norm_residual_kernel, n_next=n_next),
        grid=(m // tm,),
        in_specs=[row, row, vec] + [vec] * n_next,
        out_specs=[row] + [row] * n_next,
        out_shape=[jax.ShapeDtypeStruct((m, d), F32)] + [jax.ShapeDtypeStruct((m, d), BF16)] * n_next,
        compiler_params=_params("parallel"),
        name="norm_residual",
    )(x, y, g_post.reshape(1, d), *[g.reshape(1, d) for g in g_next])
    return outs[0], list(outs[1:])


def _conv_ln_kernel(halo_ref, cur_ref, w_ref, b_ref, g_ref, beta_ref, *rest, width, rows, cw, n_side):
    side_in, o_ref, side_out = rest[:n_side], rest[n_side], rest[n_side + 1:2 * n_side + 1]
    buf_ref, acc_ref = rest[2 * n_side + 1:]
    _run_side_casts(side_in, side_out)
    ts, ch = cur_ref.shape
    first = pl.program_id(1) == 0
    buf_ref[0:CONV_HALO, :] = jnp.where(first, 0.0, halo_ref[...])
    buf_ref[CONV_HALO:, :] = cur_ref[...]
    lead = CONV_HALO - (width - 1)
    win = rows + CONV_HALO

    def chunk(ci, carry):
        cols = pl.ds(pl.multiple_of(ci * cw, cw), cw)
        for r0 in range(0, ts, rows):
            x = buf_ref[r0:r0 + win, cols]
            acc = b_ref[:, cols] + buf_ref[CONV_HALO + r0:CONV_HALO + r0 + rows, cols] * w_ref[width - 1:width, cols]
            for b in range(8):
                xb = x if b == 0 else pltpu.roll(x, win - b, 0)
                for a in range(CONV_HALO // 8):
                    k = 8 * a + b - lead
                    if 0 <= k < width - 1:
                        acc = acc + xb[8 * a:8 * a + rows] * w_ref[k:k + 1, cols]
            acc_ref[r0:r0 + rows, cols] = acc
        return carry

    lax.fori_loop(0, ch // cw, chunk, 0)

    def norm_rows(ri, carry):
        rs = pl.ds(pl.multiple_of(ri * LN_ROWS, LN_ROWS), LN_ROWS)
        c = acc_ref[rs, :]
        mu = jnp.mean(c, axis=-1, keepdims=True)
        cc = c - mu
        var = jnp.mean(cc * cc, axis=-1, keepdims=True)
        y = (cc * lax.rsqrt(var + EPS)) * g_ref[...] + beta_ref[...]
        o_ref[rs, :] = (y * jax.nn.sigmoid(y)).astype(o_ref.dtype)
        return carry

    lax.fori_loop(0, ts // LN_ROWS, norm_rows, 0, unroll=4)


def conv_ln_silu(c0, w, b, g, beta, batch, ts, rows=64, cw=128, cast_arrays=()):
    t, ch = c0.shape
    seq = t // batch
    width = w.shape[0]
    nt = seq // ts
    hb = ts // CONV_HALO
    vec = pl.BlockSpec((1, ch), lambda bi, i: (0, 0))
    side = _side_casts(cast_arrays, batch * nt, lambda bi, i: bi * nt + i) if cast_arrays else None
    ops, s_in, s_out, s_shape = side if side else ([], [], [], [])
    outs = pl.pallas_call(
        functools.partial(_conv_ln_kernel, width=width, rows=rows, cw=cw, n_side=len(ops)),
        grid=(batch, nt),
        in_specs=[pl.BlockSpec((CONV_HALO, ch), lambda bi, i: (jnp.maximum((bi * nt + i) * hb - 1, 0), 0)),
                  pl.BlockSpec((ts, ch), lambda bi, i: (bi * nt + i, 0)),
                  pl.BlockSpec((width, ch), lambda bi, i: (0, 0)), vec, vec, vec] + s_in,
        out_specs=[pl.BlockSpec((ts, ch), lambda bi, i: (bi * nt + i, 0))] + s_out,
        out_shape=[jax.ShapeDtypeStruct((t, ch), BF16)] + s_shape,
        scratch_shapes=[pltpu.VMEM((ts + CONV_HALO, ch), F32), pltpu.VMEM((ts, ch), F32)],
        compiler_params=_params("arbitrary", "arbitrary"),
        name="conv_ln_silu",
    )(c0, c0, w, b.reshape(1, ch), g.reshape(1, ch), beta.reshape(1, ch), *ops)
    casts = list(outs[1:]) if side else [_plain_cast(*entry) for entry in cast_arrays]
    return outs[0], casts


def _mem_attn_kernel(q_ref, kv_ref, o_ref):
    width = q_ref.shape[1]
    hd = width // MEM_HEADS
    for h in range(MEM_HEADS):
        q = q_ref[:, h * hd:(h + 1) * hd]
        mk = kv_ref[:, h * hd:(h + 1) * hd]
        mv = kv_ref[:, width + h * hd:width + (h + 1) * hd]
        s = lax.dot_general(q, mk, (((1,), (1,)), ((), ())), preferred_element_type=F32)
        p = jnp.exp(s - jnp.max(s, axis=-1, keepdims=True))
        l = jnp.sum(p, axis=-1, keepdims=True)
        o = jnp.dot(p.astype(BF16), mv, preferred_element_type=F32)
        o_ref[:, h * hd:(h + 1) * hd] = (o / l).astype(o_ref.dtype)


def mem_attention(qm, mkv, batch, tm):
    t, width = qm.shape
    nt = t // batch // tm
    mt = mkv.shape[0] // batch
    return pl.pallas_call(
        _mem_attn_kernel,
        grid=(batch, nt),
        in_specs=[pl.BlockSpec((tm, width), lambda b, i: (b * nt + i, 0)),
                  pl.BlockSpec((mt, 2 * width), lambda b, i: (b, 0))],
        out_specs=pl.BlockSpec((tm, width), lambda b, i: (b * nt + i, 0)),
        out_shape=jax.ShapeDtypeStruct((t, width), BF16),
        compiler_params=_params("parallel", "arbitrary"),
        name="mem_attention",
    )(qm, mkv)


def _diff_attn_kernel(lam_ref, g_ref, q_ref, k_ref, vt_ref, o_ref, acc1_ref, acc2_ref, sa_ref, sb_ref, st_ref,
                      *, lambda_init):
    tq = q_ref.shape[0]
    d = DIFF_HEAD_DIM
    qi = pl.program_id(2)
    q = q_ref[...]
    qs = (q[:, :d], q[:, d:])
    accs = (acc1_ref, acc2_ref)
    for acc in accs:
        acc[...] = jnp.zeros_like(acc)
    st_ref[...] = jnp.where(lax.broadcasted_iota(jnp.int32, st_ref.shape, 0) % 2 == 0, -jnp.inf, 0.0)
    key = lax.broadcasted_iota(jnp.int32, (tq, tq), 0)
    qry = lax.broadcasted_iota(jnp.int32, (tq, tq), 1)
    causal = key <= qry
    nt = (((1,), (1,)), ((), ()))

    def scores(j, s_ref):
        k = k_ref[pl.ds(pl.multiple_of(j * tq, tq), tq), :]
        for c in range(2):
            s_ref[c] = lax.dot_general(k[:, c * d:(c + 1) * d], qs[c], nt, preferred_element_type=F32)

    def update(j, s_ref, masked):
        vt = vt_ref[:, pl.ds(pl.multiple_of(j * tq, tq), tq)]
        for c in range(2):
            m, l = st_ref[2 * c:2 * c + 1, :], st_ref[2 * c + 1:2 * c + 2, :]
            s = s_ref[c]
            if masked:
                s = jnp.where(causal, s, -jnp.inf)
            mn = jnp.maximum(m, jnp.max(s, axis=0, keepdims=True))
            p = jnp.exp2(s - mn)
            alpha = jnp.exp2(m - mn)
            st_ref[2 * c:2 * c + 1, :] = mn
            st_ref[2 * c + 1:2 * c + 2, :] = alpha * l + jnp.sum(p, axis=0, keepdims=True)
            accs[c][...] = accs[c][...] * alpha + jnp.dot(vt, p.astype(BF16), preferred_element_type=F32)

    scores(0, sa_ref)

    def pairs(j, count):
        for _ in range(count):
            scores(j + 1, sb_ref)
            update(j, sa_ref, False)
            scores(j + 2, sa_ref)
            update(j + 1, sb_ref, False)
            j = j + 2

    def quad(i, carry):
        pairs(4 * i, 2)
        return carry

    def pair(i, carry):
        pairs(4 * (qi // 4) + 2 * i, 1)
        return carry

    lax.fori_loop(0, qi // 4, quad, 0)
    lax.fori_loop(0, (qi % 4) // 2, pair, 0)

    @pl.when(qi % 2 == 1)
    def _():
        scores(qi, sb_ref)
        update(qi - 1, sa_ref, False)
        update(qi, sb_ref, True)

    @pl.when(qi % 2 == 0)
    def _():
        update(qi, sa_ref, True)

    l1, l2 = st_ref[1:2, :], st_ref[3:4, :]
    lm = lam_ref[...]
    lam = (jnp.exp(jnp.sum(lm[0:1] * lm[1:2], axis=-1, keepdims=True))
           - jnp.exp(jnp.sum(lm[2:3] * lm[3:4], axis=-1, keepdims=True)) + lambda_init)
    o = acc1_ref[...] / l1 - lam * (acc2_ref[...] / l2)
    inv = lax.rsqrt(jnp.mean(o * o, axis=0, keepdims=True) + EPS)
    o = (o * inv) * g_ref[...] * (1.0 - lambda_init)
    o_ref[...] = o.T.astype(o_ref.dtype)


def diff_attention(q, k, vt, lam_params, subln, batch, tq, lambda_init):
    t, width = q.shape
    seq = t // batch
    hw = 2 * DIFF_HEAD_DIM
    heads = width // hw
    nq = seq // tq
    return pl.pallas_call(
        functools.partial(_diff_attn_kernel, lambda_init=lambda_init),
        grid=(batch, heads, nq),
        in_specs=[pl.BlockSpec((4, DIFF_HEAD_DIM), lambda b, h, i: (0, 0)),
                  pl.BlockSpec((hw, 1), lambda b, h, i: (0, 0)),
                  pl.BlockSpec((tq, hw), lambda b, h, i: (b * nq + i, h)),
                  pl.BlockSpec((seq, hw), lambda b, h, i: (b, h)),
                  pl.BlockSpec((hw, seq), lambda b, h, i: (h, b))],
        out_specs=pl.BlockSpec((tq, hw), lambda b, h, i: (b * nq + i, h)),
        out_shape=jax.ShapeDtypeStruct((t, width), BF16),
        scratch_shapes=[pltpu.VMEM((hw, tq), F32), pltpu.VMEM((hw, tq), F32),
                        pltpu.VMEM((2, tq, tq), F32), pltpu.VMEM((2, tq, tq), F32), pltpu.VMEM((8, tq), F32)],
        compiler_params=_params("parallel", "parallel", "arbitrary"),
        name="diff_attention",
    )(lam_params, subln.reshape(hw, 1), q, k, vt)


def _tile(n, want):
    t = min(n, want)
    while n % t:
        t //= 2
    return t


def kernel(x, mem, positions, a_norm_pre, a_norm_post, a_mem_norm, a_w_in, a_conv_w, a_conv_b, a_ln_g, a_ln_b,
           a_w_mem_kv, a_w_out, kv_norm, w_kv, b_norm_pre, b_norm_post, b_mem_norm, b_w_in, b_lambda, b_subln,
           b_w_mem_kv, b_w_out, mlp_norm_pre, mlp_norm_post, w_up, w_down):
    batch, seq, d = x.shape
    t = batch * seq
    n_a = a_norm_pre.shape[0]
    depth = n_a + b_norm_pre.shape[0]
    conv_ch = a_conv_w.shape[2]
    mem_width = a_w_in.shape[2] - 2 * conv_ch
    qk_width = b_w_in.shape[2] - mem_width
    mem_scale = float((mem_width // MEM_HEADS) ** -0.5)

    tm = _tile(t, 1024)
    tn = 1024
    tn_glu = 512
    te = _tile(t, 256)
    mem2 = mem.reshape(-1, d)
    tmem = _tile(mem2.shape[0], 512)

    xr = x.reshape(t, d)
    tables = rope_tables(positions, _tile(t, 1024))
    h = rmsnorm_bf16(xr, a_norm_pre[0] if n_a else b_norm_pre[0], te)
    k_sh = v_sh = None
    mlp_w = None
    pre = {}

    def bf16_weight(name, stacked, idx):
        return pre[name] if name in pre else stacked[idx].astype(BF16)

    for layer in range(depth):
        last = layer == depth - 1
        if layer < n_a:
            i = layer
            w_in = bf16_weight("w_in", a_w_in, i)
            c0 = matmul_glu(h, w_in, conv_ch, tm, tn_glu)
            qm = matmul(h, w_in, tm, tn, scale=mem_scale, col0=2 * conv_ch)
            cast = () if mlp_w else ((w_up, layer), (w_down, layer))
            main, casts = conv_ln_silu(c0, a_conv_w[i], a_conv_b[i], a_ln_g[i], a_ln_b[i], batch, _tile(seq, 128),
                                       cast_arrays=cast)
            mlp_w = mlp_w or casts
            mem_n = rmsnorm_bf16(mem2, a_mem_norm[i], _tile(mem2.shape[0], 256))
            mkv = matmul(mem_n, bf16_weight("w_mem_kv", a_w_mem_kv, i), tmem, tn)
            w_out = bf16_weight("w_out", a_w_out, i)
            g_post = a_norm_post[i]
        else:
            j = layer - n_a
            lambda_init = 0.8 - 0.6 * math.exp(-0.3 * layer)
            w_in = bf16_weight("w_in", b_w_in, j)
            q = matmul_rope(h, w_in, tables, tm, tn, scale=DIFF_HEAD_DIM ** -0.5 * LOG2E, ncols=qk_width)
            qm = matmul(h, w_in, tm, tn, scale=mem_scale, col0=qk_width)
            main = diff_attention(q, k_sh, v_sh, b_lambda[j], b_subln[j], batch, _tile(seq, 512), lambda_init)
            mem_n = rmsnorm_bf16(mem2, b_mem_norm[j], _tile(mem2.shape[0], 256))
            mkv = matmul(mem_n, bf16_weight("w_mem_kv", b_w_mem_kv, j), tmem, tn)
            w_out = bf16_weight("w_out", b_w_out, j)
            g_post = b_norm_post[j]

        m_out = mem_attention(qm, mkv, batch, _tile(seq, 512))
        y = matmul_concat(main, m_out, w_out, tm, tn)
        xr, (h,) = norm_residual(xr, y, g_post, [mlp_norm_pre[layer]], te)

        mlp_w = mlp_w or [w_up[layer].astype(BF16), w_down[layer].astype(BF16)]
        nxt, names = [], []
        if not last:
            nl = layer + 1
            proj = (a_w_in, a_w_mem_kv, a_w_out, nl) if nl < n_a else (b_w_in, b_w_mem_kv, b_w_out, nl - n_a)
            nxt = [(w_up, nl), (w_down, nl)] + [(w, proj[3]) for w in proj[:3]]
            names = ["w_in", "w_mem_kv", "w_out"]
        if layer == n_a - 1:
            nxt.append((w_kv[None], 0, qk_width))
            names.append("w_k")
        y, casts = mlp(h, mlp_w[0], mlp_w[1], tm, 512, cast_arrays=nxt)
        mlp_w = casts[:2] if not last else None
        pre = dict(zip(names, casts[2:] if not last else casts))
        g_next = []
        if layer == n_a - 1:
            g_next.append(kv_norm)
        if not last:
            g_next.append(a_norm_pre[layer + 1] if layer + 1 < n_a else b_norm_pre[layer + 1 - n_a])
        xr, hs = norm_residual(xr, y, mlp_norm_post[layer], g_next, te)
        if layer == n_a - 1:
            k_sh = matmul_rope(hs[0], pre.pop("w_k"), tables, tm, tn)
            v_sh = matmul_transposed_out(hs[0], w_kv[:, qk_width:].T.astype(BF16), tm, 512)
        if not last:
            h = hs[-1]

    return xr.reshape(batch, seq, d)
```

```python
import functools
import math

import jax
import jax.numpy as jnp
from jax import lax
from jax.experimental import pallas as pl
from jax.experimental.pallas import tpu as pltpu

F32 = jnp.float32
BF16 = jnp.bfloat16

EPS = 1e-6
ROPE_THETA = 500000.0
LANES = 128
BF16_SUBLANES = 16
MEM_HEADS = 4
DIFF_HEAD_DIM = 128
ROT_DIM = DIFF_HEAD_DIM // 4
LN_ROWS = 16
CONV_HALO = 32
VMEM_LIMIT = 56 * 1024 * 1024
MLP_VMEM_LIMIT = 60 * 1024 * 1024
LOG2E = 1.4426950408889634


def _params(*sem):
    return pltpu.CompilerParams(dimension_semantics=sem, vmem_limit_bytes=VMEM_LIMIT)


def _rms_scale(t):
    return lax.rsqrt(jnp.mean(t * t, axis=-1, keepdims=True) + EPS)


def _rmsnorm_kernel(x_ref, g_ref, o_ref):
    x = x_ref[...]
    o_ref[...] = ((x * _rms_scale(x)) * g_ref[...]).astype(o_ref.dtype)


def rmsnorm_bf16(x, g, tm):
    m, d = x.shape
    return pl.pallas_call(
        _rmsnorm_kernel,
        grid=(m // tm,),
        in_specs=[pl.BlockSpec((tm, d), lambda i: (i, 0)), pl.BlockSpec((1, d), lambda i: (0, 0))],
        out_specs=pl.BlockSpec((tm, d), lambda i: (i, 0)),
        out_shape=jax.ShapeDtypeStruct((m, d), BF16),
        compiler_params=_params("parallel"),
        name="rmsnorm",
    )(x, g.reshape(1, d))


def _rope_table_kernel(pos_ref, freq_ref, c_ref, sa_ref, sb_ref):
    ang = pos_ref[...] * freq_ref[...]
    lane = lax.broadcasted_iota(jnp.int32, ang.shape, 1)
    sin = jnp.sin(ang)
    c_ref[...] = jnp.where(lane < ROT_DIM, jnp.cos(ang), 1.0)
    sa_ref[...] = jnp.where(lane < ROT_DIM // 2, -sin, 0.0)
    sb_ref[...] = jnp.where((lane >= ROT_DIM // 2) & (lane < ROT_DIM), sin, 0.0)


def rope_tables(positions, tm):
    t = positions.size
    pos = jnp.broadcast_to(positions.reshape(t, 1).astype(F32), (t, LANES))
    inv_freq = jnp.power(jnp.float32(ROPE_THETA), -jnp.arange(0, ROT_DIM, 2, dtype=F32) / ROT_DIM)
    freq = jnp.concatenate([inv_freq, inv_freq, jnp.zeros((LANES - ROT_DIM,), F32)]).reshape(1, LANES)
    spec = pl.BlockSpec((tm, LANES), lambda i: (i, 0))
    shp = jax.ShapeDtypeStruct((t, LANES), F32)
    return pl.pallas_call(
        _rope_table_kernel,
        grid=(t // tm,),
        in_specs=[spec, pl.BlockSpec((1, LANES), lambda i: (0, 0))],
        out_specs=[spec, spec, spec],
        out_shape=[shp, shp, shp],
        compiler_params=_params("parallel"),
        name="rope_tables",
    )(pos, freq)


def _mm_kernel(a_ref, w_ref, o_ref, *, scale):
    acc = jnp.dot(a_ref[...], w_ref[...], preferred_element_type=F32)
    if scale != 1.0:
        acc = acc * scale
    o_ref[...] = acc.astype(o_ref.dtype)


def _col_window(w, tn, col0, ncols):
    n = w.shape[1] - col0 if ncols is None else ncols
    tn = _tile(math.gcd(n, col0) if col0 else n, tn)
    return n, tn, col0 // tn


def matmul(a, w, tm, tn, scale=1.0, out_dtype=BF16, col0=0, ncols=None):
    m, k = a.shape
    n, tn, j0 = _col_window(w, tn, col0, ncols)
    return pl.pallas_call(
        functools.partial(_mm_kernel, scale=scale),
        grid=(m // tm, n // tn),
        in_specs=[pl.BlockSpec((tm, k), lambda i, j: (i, 0)), pl.BlockSpec((k, tn), lambda i, j: (0, j + j0))],
        out_specs=pl.BlockSpec((tm, tn), lambda i, j: (i, j)),
        out_shape=jax.ShapeDtypeStruct((m, n), out_dtype),
        compiler_params=_params("parallel", "arbitrary"),
        name="matmul",
    )(a, w)


def _mm_nt_kernel(wt_ref, a_ref, o_ref):
    acc = lax.dot_general(wt_ref[...], a_ref[...], (((1,), (1,)), ((), ())), preferred_element_type=F32)
    o_ref[...] = acc.astype(o_ref.dtype)


def matmul_transposed_out(a, wt, tm, tn):
    m, k = a.shape
    n = wt.shape[0]
    tn = _tile(n, tn)
    return pl.pallas_call(
        _mm_nt_kernel,
        grid=(m // tm, n // tn),
        in_specs=[pl.BlockSpec((tn, k), lambda i, j: (j, 0)), pl.BlockSpec((tm, k), lambda i, j: (i, 0))],
        out_specs=pl.BlockSpec((tn, tm), lambda i, j: (j, i)),
        out_shape=jax.ShapeDtypeStruct((n, m), BF16),
        compiler_params=_params("parallel", "arbitrary"),
        name="matmul_transposed_out",
    )(wt, a)


def _mm_rope_kernel(a_ref, w_ref, c_ref, sa_ref, sb_ref, o_ref, *, scale):
    acc = jnp.dot(a_ref[...], w_ref[...], preferred_element_type=F32)
    c = c_ref[...] * scale
    sa = sa_ref[...] * scale
    sb = sb_ref[...] * scale
    for g in range(acc.shape[1] // LANES):
        t = acc[:, g * LANES:(g + 1) * LANES]
        r = t * c + pltpu.roll(t, LANES - ROT_DIM // 2, 1) * sa + pltpu.roll(t, ROT_DIM // 2, 1) * sb
        o_ref[:, g * LANES:(g + 1) * LANES] = r.astype(o_ref.dtype)


def matmul_rope(a, w, tables, tm, tn, scale=1.0, ncols=None):
    m, k = a.shape
    n, tn, _ = _col_window(w, tn, 0, ncols)
    tab = pl.BlockSpec((tm, LANES), lambda i, j: (i, 0))
    return pl.pallas_call(
        functools.partial(_mm_rope_kernel, scale=scale),
        grid=(m // tm, n // tn),
        in_specs=[pl.BlockSpec((tm, k), lambda i, j: (i, 0)), pl.BlockSpec((k, tn), lambda i, j: (0, j)),
                  tab, tab, tab],
        out_specs=pl.BlockSpec((tm, tn), lambda i, j: (i, j)),
        out_shape=jax.ShapeDtypeStruct((m, n), BF16),
        compiler_params=_params("parallel", "arbitrary"),
        name="matmul_rope",
    )(a, w, *tables)


def _mm_glu_kernel(a_ref, wu_ref, wg_ref, o_ref):
    a = a_ref[...]
    u = jnp.dot(a, wu_ref[...], preferred_element_type=F32)
    g = jnp.dot(a, wg_ref[...], preferred_element_type=F32)
    o_ref[...] = u * jax.nn.sigmoid(g)


def matmul_glu(a, w, n, tm, tn):
    m, k = a.shape
    tn = _tile(n, tn)
    nj = n // tn
    return pl.pallas_call(
        _mm_glu_kernel,
        grid=(m // tm, nj),
        in_specs=[pl.BlockSpec((tm, k), lambda i, j: (i, 0)), pl.BlockSpec((k, tn), lambda i, j: (0, j)),
                  pl.BlockSpec((k, tn), lambda i, j: (0, j + nj))],
        out_specs=pl.BlockSpec((tm, tn), lambda i, j: (i, j)),
        out_shape=jax.ShapeDtypeStruct((m, n), F32),
        compiler_params=_params("parallel", "arbitrary"),
        name="matmul_glu",
    )(a, w, w)


def _mm2_kernel(a1_ref, a2_ref, w1_ref, w2_ref, o_ref):
    acc = (jnp.dot(a1_ref[...], w1_ref[...], preferred_element_type=F32)
           + jnp.dot(a2_ref[...], w2_ref[...], preferred_element_type=F32))
    o_ref[...] = acc.astype(o_ref.dtype)


def matmul_concat(a1, a2, w, tm, tn):
    m, k1 = a1.shape
    k2 = a2.shape[1]
    n = w.shape[1]
    tn = _tile(n, tn)
    assert k1 % k2 == 0 and w.shape[0] == k1 + k2
    return pl.pallas_call(
        _mm2_kernel,
        grid=(m // tm, n // tn),
        in_specs=[pl.BlockSpec((tm, k1), lambda i, j: (i, 0)), pl.BlockSpec((tm, k2), lambda i, j: (i, 0)),
                  pl.BlockSpec((k1, tn), lambda i, j: (0, j)), pl.BlockSpec((k2, tn), lambda i, j: (k1 // k2, j))],
        out_specs=pl.BlockSpec((tm, tn), lambda i, j: (i, j)),
        out_shape=jax.ShapeDtypeStruct((m, n), BF16),
        compiler_params=_params("parallel", "arbitrary"),
        name="matmul_concat",
    )(a1, a2, w, w)


def _plain_cast(stacked, layer, ncols=None):
    return stacked[layer, :, :ncols].astype(BF16)


def _side_cast_specs(stacked, layer, steps, step_id, ncols=None):
    rows_total, cols = stacked.shape[1:]
    cols = ncols or cols
    if rows_total % steps or stacked.shape[2] % cols:
        return None
    rows = rows_total // steps
    hold = max(1, BF16_SUBLANES // rows)
    if (rows * hold) % BF16_SUBLANES or rows_total % (rows * hold):
        return None
    in_spec = pl.BlockSpec((None, rows * hold, cols), lambda *g: (layer, step_id(*g) // hold, 0))
    out_spec = pl.BlockSpec((rows * hold, cols), lambda *g: (step_id(*g) // hold, 0))
    return in_spec, out_spec, jax.ShapeDtypeStruct((rows_total, cols), BF16)


def _side_casts(arrays, steps, step_id):
    specs = [_side_cast_specs(e[0], e[1], steps, step_id, *e[2:]) for e in arrays]
    if any(s is None for s in specs):
        return None
    return ([e[0] for e in arrays], [s[0] for s in specs], [s[1] for s in specs], [s[2] for s in specs])


def _run_side_casts(in_refs, out_refs):
    for i_ref, o_ref in zip(in_refs, out_refs):
        o_ref[...] = i_ref[...].astype(o_ref.dtype)


def _mlp_kernel(h_ref, wu_ref, wd_ref, *rest, n_side):
    side_in, o_ref, side_out, acc_ref = rest[:n_side], rest[n_side], rest[n_side + 1:-1], rest[-1]

    @pl.when(pl.program_id(1) == 0)
    def _():
        acc_ref[...] = jnp.zeros_like(acc_ref)

    a = jnp.dot(h_ref[...], wu_ref[...], preferred_element_type=F32)
    a = jnp.maximum(a, 0.0)
    a = (a * a).astype(BF16)
    acc_ref[...] += jnp.dot(a, wd_ref[...], preferred_element_type=F32)
    _run_side_casts(side_in, side_out)

    @pl.when(pl.program_id(1) == pl.num_programs(1) - 1)
    def _():
        o_ref[...] = acc_ref[...].astype(o_ref.dtype)


def mlp(h, w_up, w_down, tm, tf, cast_arrays=()):
    m, d = h.shape
    f = w_up.shape[1]
    nj = f // tf
    side = _side_casts(cast_arrays, (m // tm) * nj, lambda i, j: i * nj + j) if cast_arrays else None
    ops, s_in, s_out, s_shape = side if side else ([], [], [], [])
    outs = pl.pallas_call(
        functools.partial(_mlp_kernel, n_side=len(ops)),
        grid=(m // tm, nj),
        in_specs=[pl.BlockSpec((tm, d), lambda i, j: (i, 0), pipeline_mode=pl.Buffered(1 if ops else 2)),
                  pl.BlockSpec((d, tf), lambda i, j: (0, j)), pl.BlockSpec((tf, d), lambda i, j: (j, 0))] + s_in,
        out_specs=[pl.BlockSpec((tm, d), lambda i, j: (i, 0), pipeline_mode=pl.Buffered(1))] + s_out,
        out_shape=[jax.ShapeDtypeStruct((m, d), BF16)] + s_shape,
        scratch_shapes=[pltpu.VMEM((tm, d), F32)],
        compiler_params=pltpu.CompilerParams(dimension_semantics=("arbitrary", "arbitrary"),
                                             vmem_limit_bytes=MLP_VMEM_LIMIT),
        name="mlp",
    )(h, w_up, w_down, *ops)
    casts = list(outs[1:]) if side else [_plain_cast(*entry) for entry in cast_arrays]
    return outs[0], casts


def _norm_residual_kernel(x_ref, y_ref, gp_ref, *rest, n_next):
    g_refs, x_out, h_outs = rest[:n_next], rest[n_next], rest[n_next + 1:]
    y = y_ref[...].astype(F32)
    x = x_ref[...] + (y * _rms_scale(y)) * gp_ref[...]
    x_out[...] = x
    if n_next:
        xn = x * _rms_scale(x)
        for g_ref, h_out in zip(g_refs, h_outs):
            h_out[...] = (xn * g_ref[...]).astype(h_out.dtype)


def norm_residual(x, y, g_post, g_next, tm):
    m, d = x.shape
    row = pl.BlockSpec((tm, d), lambda i: (i, 0))
    vec = pl.BlockSpec((1, d), lambda i: (0, 0))
    n_next = len(g_next)
    outs = pl.pallas_call(
        functools.partial(_norm_residual_kernel, n_next=n_next),
        grid=(m // tm,),
        in_specs=[row, row, vec] + [vec] * n_next,
        out_specs=[row] + [row] * n_next,
        out_shape=[jax.ShapeDtypeStruct((m, d), F32)] + [jax.ShapeDtypeStruct((m, d), BF16)] * n_next,
        compiler_params=_params("parallel"),
        name="norm_residual",
    )(x, y, g_post.reshape(1, d), *[g.reshape(1, d) for g in g_next])
    return outs[0], list(outs[1:])


def _conv_ln_kernel(halo_ref, cur_ref, w_ref, b_ref, g_ref, beta_ref, *rest, width, rows, cw, n_side):
    side_in, o_ref, side_out = rest[:n_side], rest[n_side], rest[n_side + 1:2 * n_side + 1]
    buf_ref, acc_ref = rest[2 * n_side + 1:]
    _run_side_casts(side_in, side_out)
    ts, ch = cur_ref.shape
    first = pl.program_id(1) == 0
    buf_ref[0:CONV_HALO, :] = jnp.where(first, 0.0, halo_ref[...])
    buf_ref[CONV_HALO:, :] = cur_ref[...]
    lead = CONV_HALO - (width - 1)
    win = rows + CONV_HALO

    def chunk(ci, carry):
        cols = pl.ds(pl.multiple_of(ci * cw, cw), cw)
        for r0 in range(0, ts, rows):
            x = buf_ref[r0:r0 + win, cols]
            acc = b_ref[:, cols] + buf_ref[CONV_HALO + r0:CONV_HALO + r0 + rows, cols] * w_ref[width - 1:width, cols]
            for b in range(8):
                xb = x if b == 0 else pltpu.roll(x, win - b, 0)
                for a in range(CONV_HALO // 8):
                    k = 8 * a + b - lead
                    if 0 <= k < width - 1:
                        acc = acc + xb[8 * a:8 * a + rows] * w_ref[k:k + 1, cols]
            acc_ref[r0:r0 + rows, cols] = acc
        return carry

    lax.fori_loop(0, ch // cw, chunk, 0)

    def norm_rows(ri, carry):
        rs = pl.ds(pl.multiple_of(ri * LN_ROWS, LN_ROWS), LN_ROWS)
        c = acc_ref[rs, :]
        mu = jnp.mean(c, axis=-1, keepdims=True)
        cc = c - mu
        var = jnp.mean(cc * cc, axis=-1, keepdims=True)
        y = (cc * lax.rsqrt(var + EPS)) * g_ref[...] + beta_ref[...]
        o_ref[rs, :] = (y * jax.nn.sigmoid(y)).astype(o_ref.dtype)
        return carry

    lax.fori_loop(0, ts // LN_ROWS, norm_rows, 0, unroll=4)


def conv_ln_silu(c0, w, b, g, beta, batch, ts, rows=64, cw=128, cast_arrays=()):
    t, ch = c0.shape
    seq = t // batch
    width = w.shape[0]
    nt = seq // ts
    hb = ts // CONV_HALO
    vec = pl.BlockSpec((1, ch), lambda bi, i: (0, 0))
    side = _side_casts(cast_arrays, batch * nt, lambda bi, i: bi * nt + i) if cast_arrays else None
    ops, s_in, s_out, s_shape = side if side else ([], [], [], [])
    outs = pl.pallas_call(
        functools.partial(_conv_ln_kernel, width=width, rows=rows, cw=cw, n_side=len(ops)),
        grid=(batch, nt),
        in_specs=[pl.BlockSpec((CONV_HALO, ch), lambda bi, i: (jnp.maximum((bi * nt + i) * hb - 1, 0), 0)),
                  pl.BlockSpec((ts, ch), lambda bi, i: (bi * nt + i, 0)),
                  pl.BlockSpec((width, ch), lambda bi, i: (0, 0)), vec, vec, vec] + s_in,
        out_specs=[pl.BlockSpec((ts, ch), lambda bi, i: (bi * nt + i, 0))] + s_out,
        out_shape=[jax.ShapeDtypeStruct((t, ch), BF16)] + s_shape,
        scratch_shapes=[pltpu.VMEM((ts + CONV_HALO, ch), F32), pltpu.VMEM((ts, ch), F32)],
        compiler_params=_params("arbitrary", "arbitrary"),
        name="conv_ln_silu",
    )(c0, c0, w, b.reshape(1, ch), g.reshape(1, ch), beta.reshape(1, ch), *ops)
    casts = list(outs[1:]) if side else [_plain_cast(*entry) for entry in cast_arrays]
    return outs[0], casts


def _mem_attn_kernel(q_ref, kv_ref, o_ref):
    width = q_ref.shape[1]
    hd = width // MEM_HEADS
    for h in range(MEM_HEADS):
        q = q_ref[:, h * hd:(h + 1) * hd]
        mk = kv_ref[:, h * hd:(h + 1) * hd]
        mv = kv_ref[:, width + h * hd:width + (h + 1) * hd]
        s = lax.dot_general(q, mk, (((1,), (1,)), ((), ())), preferred_element_type=F32)
        p = jnp.exp(s - jnp.max(s, axis=-1, keepdims=True))
        l = jnp.sum(p, axis=-1, keepdims=True)
        o = jnp.dot(p.astype(BF16), mv, preferred_element_type=F32)
        o_ref[:, h * hd:(h + 1) * hd] = (o / l).astype(o_ref.dtype)


def mem_attention(qm, mkv, batch, tm):
    t, width = qm.shape
    nt = t // batch // tm
    mt = mkv.shape[0] // batch
    return pl.pallas_call(
        _mem_attn_kernel,
        grid=(batch, nt),
        in_specs=[pl.BlockSpec((tm, width), lambda b, i: (b * nt + i, 0)),
                  pl.BlockSpec((mt, 2 * width), lambda b, i: (b, 0))],
        out_specs=pl.BlockSpec((tm, width), lambda b, i: (b * nt + i, 0)),
        out_shape=jax.ShapeDtypeStruct((t, width), BF16),
        compiler_params=_params("parallel", "arbitrary"),
        name="mem_attention",
    )(qm, mkv)


def _diff_attn_kernel(lam_ref, g_ref, q_ref, k_ref, vt_ref, o_ref, acc1_ref, acc2_ref, sa_ref, sb_ref, st_ref,
                      *, lambda_init):
    tq = q_ref.shape[0]
    d = DIFF_HEAD_DIM
    qi = pl.program_id(2)
    q = q_ref[...]
    qs = (q[:, :d], q[:, d:])
    accs = (acc1_ref, acc2_ref)
    for acc in accs:
        acc[...] = jnp.zeros_like(acc)
    st_ref[...] = jnp.where(lax.broadcasted_iota(jnp.int32, st_ref.shape, 0) % 2 == 0, -jnp.inf, 0.0)
    key = lax.broadcasted_iota(jnp.int32, (tq, tq), 0)
    qry = lax.broadcasted_iota(jnp.int32, (tq, tq), 1)
    causal = key <= qry
    nt = (((1,), (1,)), ((), ()))

    def scores(j, s_ref):
        k = k_ref[pl.ds(pl.multiple_of(j * tq, tq), tq), :]
        for c in range(2):
            s_ref[c] = lax.dot_general(k[:, c * d:(c + 1) * d], qs[c], nt, preferred_element_type=F32)

    def update(j, s_ref, masked):
        vt = vt_ref[:, pl.ds(pl.multiple_of(j * tq, tq), tq)]
        for c in range(2):
            m, l = st_ref[2 * c:2 * c + 1, :], st_ref[2 * c + 1:2 * c + 2, :]
            s = s_ref[c]
            if masked:
                s = jnp.where(causal, s, -jnp.inf)
            mn = jnp.maximum(m, jnp.max(s, axis=0, keepdims=True))
            p = jnp.exp2(s - mn)
            alpha = jnp.exp2(m - mn)
            st_ref[2 * c:2 * c + 1, :] = mn
            st_ref[2 * c + 1:2 * c + 2, :] = alpha * l + jnp.sum(p, axis=0, keepdims=True)
            accs[c][...] = accs[c][...] * alpha + jnp.dot(vt, p.astype(BF16), preferred_element_type=F32)

    scores(0, sa_ref)

    def pairs(j, count):
        for _ in range(count):
            scores(j + 1, sb_ref)
            update(j, sa_ref, False)
            scores(j + 2, sa_ref)
            update(j + 1, sb_ref, False)
            j = j + 2

    def quad(i, carry):
        pairs(4 * i, 2)
        return carry

    def pair(i, carry):
        pairs(4 * (qi // 4) + 2 * i, 1)
        return carry

    lax.fori_loop(0, qi // 4, quad, 0)
    lax.fori_loop(0, (qi % 4) // 2, pair, 0)

    @pl.when(qi % 2 == 1)
    def _():
        scores(qi, sb_ref)
        update(qi - 1, sa_ref, False)
        update(qi, sb_ref, True)

    @pl.when(qi % 2 == 0)
    def _():
        update(qi, sa_ref, True)

    l1, l2 = st_ref[1:2, :], st_ref[3:4, :]
    lm = lam_ref[...]
    lam = (jnp.exp(jnp.sum(lm[0:1] * lm[1:2], axis=-1, keepdims=True))
           - jnp.exp(jnp.sum(lm[2:3] * lm[3:4], axis=-1, keepdims=True)) + lambda_init)
    o = acc1_ref[...] / l1 - lam * (acc2_ref[...] / l2)
    inv = lax.rsqrt(jnp.mean(o * o, axis=0, keepdims=True) + EPS)
    o = (o * inv) * g_ref[...] * (1.0 - lambda_init)
    o_ref[...] = o.T.astype(o_ref.dtype)


def diff_attention(q, k, vt, lam_params, subln, batch, tq, lambda_init):
    t, width = q.shape
    seq = t // batch
    hw = 2 * DIFF_HEAD_DIM
    heads = width // hw
    nq = seq // tq
    return pl.pallas_call(
        functools.partial(_diff_attn_kernel, lambda_init=lambda_init),
        grid=(batch, heads, nq),
        in_specs=[pl.BlockSpec((4, DIFF_HEAD_DIM), lambda b, h, i: (0, 0)),
                  pl.BlockSpec((hw, 1), lambda b, h, i: (0, 0)),
                  pl.BlockSpec((tq, hw), lambda b, h, i: (b * nq + i, h)),
                  pl.BlockSpec((seq, hw), lambda b, h, i: (b, h)),
                  pl.BlockSpec((hw, seq), lambda b, h, i: (h, b))],
        out_specs=pl.BlockSpec((tq, hw), lambda b, h, i: (b * nq + i, h)),
        out_shape=jax.ShapeDtypeStruct((t, width), BF16),
        scratch_shapes=[pltpu.VMEM((hw, tq), F32), pltpu.VMEM((hw, tq), F32),
                        pltpu.VMEM((2, tq, tq), F32), pltpu.VMEM((2, tq, tq), F32), pltpu.VMEM((8, tq), F32)],
        compiler_params=_params("parallel", "parallel", "arbitrary"),
        name="diff_attention",
    )(lam_params, subln.reshape(hw, 1), q, k, vt)


def _tile(n, want):
    t = min(n, want)
    while n % t:
        t //= 2
    return t


def kernel(x, mem, positions, a_norm_pre, a_norm_post, a_mem_norm, a_w_in, a_conv_w, a_conv_b, a_ln_g, a_ln_b,
           a_w_mem_kv, a_w_out, kv_norm, w_kv, b_norm_pre, b_norm_post, b_mem_norm, b_w_in, b_lambda, b_subln,
           b_w_mem_kv, b_w_out, mlp_norm_pre, mlp_norm_post, w_up, w_down):
    batch, seq, d = x.shape
    t = batch * seq
    n_a = a_norm_pre.shape[0]
    depth = n_a + b_norm_pre.shape[0]
    conv_ch = a_conv_w.shape[2]
    mem_width = a_w_in.shape[2] - 2 * conv_ch
    qk_width = b_w_in.shape[2] - mem_width
    mem_scale = float((mem_width // MEM_HEADS) ** -0.5)

    tm = _tile(t, 1024)
    tn = 1024
    tn_glu = 512
    te = _tile(t, 256)
    mem2 = mem.reshape(-1, d)
    tmem = _tile(mem2.shape[0], 512)

    xr = x.reshape(t, d)
    tables = rope_tables(positions, _tile(t, 1024))
    h = rmsnorm_bf16(xr, a_norm_pre[0] if n_a else b_norm_pre[0], te)
    k_sh = v_sh = None
    mlp_w = None
    pre = {}

    def bf16_weight(name, stacked, idx):
        return pre[name] if name in pre else stacked[idx].astype(BF16)

    for layer in range(depth):
        last = layer == depth - 1
        if layer < n_a:
            i = layer
            w_in = bf16_weight("w_in", a_w_in, i)
            c0 = matmul_glu(h, w_in, conv_ch, tm, tn_glu)
            qm = matmul(h, w_in, tm, tn, scale=mem_scale, col0=2 * conv_ch)
            cast = () if mlp_w else ((w_up, layer), (w_down, layer))
            main, casts = conv_ln_silu(c0, a_conv_w[i], a_conv_b[i], a_ln_g[i], a_ln_b[i], batch, _tile(seq, 256),
                                       cast_arrays=cast)
            mlp_w = mlp_w or casts
            mem_n = rmsnorm_bf16(mem2, a_mem_norm[i], _tile(mem2.shape[0], 256))
            mkv = matmul(mem_n, bf16_weight("w_mem_kv", a_w_mem_kv, i), tmem, tn)
            w_out = bf16_weight("w_out", a_w_out, i)
            g_post = a_norm_post[i]
        else:
            j = layer - n_a
            lambda_init = 0.8 - 0.6 * math.exp(-0.3 * layer)
            w_in = bf16_weight("w_in", b_w_in, j)
            q = matmul_rope(h, w_in, tables, tm, tn, scale=DIFF_HEAD_DIM ** -0.5 * LOG2E, ncols=qk_width)
            qm = matmul(h, w_in, tm, tn, scale=mem_scale, col0=qk_width)
            main = diff_attention(q, k_sh, v_sh, b_lambda[j], b_subln[j], batch, _tile(seq, 512), lambda_init)
            mem_n = rmsnorm_bf16(mem2, b_mem_norm[j], _tile(mem2.shape[0], 256))
            mkv = matmul(mem_n, bf16_weight("w_mem_kv", b_w_mem_kv, j), tmem, tn)
            w_out = bf16_weight("w_out", b_w_out, j)
            g_post = b_norm_post[j]

        m_out = mem_attention(qm, mkv, batch, _tile(seq, 512))
        y = matmul_concat(main, m_out, w_out, tm, tn)
        xr, (h,) = norm_residual(xr, y, g_post, [mlp_norm_pre[layer]], te)

        mlp_w = mlp_w or [w_up[layer].astype(BF16), w_down[layer].astype(BF16)]
        nxt, names = [], []
        if not last:
            nl = layer + 1
            proj = (a_w_in, a_w_mem_kv, a_w_out, nl) if nl < n_a else (b_w_in, b_w_mem_kv, b_w_out, nl - n_a)
            nxt = [(w_up, nl), (w_down, nl)] + [(w, proj[3]) for w in proj[:3]]
            names = ["w_in", "w_mem_kv", "w_out"]
        if layer == n_a - 1:
            nxt.append((w_kv[None], 0, qk_width))
            names.append("w_k")
        y, casts = mlp(h, mlp_w[0], mlp_w[1], tm, 512, cast_arrays=nxt)
        mlp_w = casts[:2] if not last else None
        pre = dict(zip(names, casts[2:] if not last else casts))
        g_next = []
        if layer == n_a - 1:
            g_next.append(kv_norm)
        if not last:
            g_next.append(a_norm_pre[layer + 1] if layer + 1 < n_a else b_norm_pre[layer + 1 - n_a])
        xr, hs = norm_residual(xr, y, mlp_norm_post[layer], g_next, te)
        if layer == n_a - 1:
            k_sh = matmul_rope(hs[0], pre.pop("w_k"), tables, tm, tn)
            v_sh = matmul_transposed_out(hs[0], w_kv[:, qk_width:].T.astype(BF16), tm, 512)
        if not last:
            h = hs[-1]

    return xr.reshape(batch, seq, d)
```

```python
import functools
import math

import jax
import jax.numpy as jnp
from jax import lax
from jax.experimental import pallas as pl
from jax.experimental.pallas import tpu as pltpu

F32 = jnp.float32
BF16 = jnp.bfloat16

EPS = 1e-6
ROPE_THETA = 500000.0
LANES = 128
BF16_SUBLANES = 16
MEM_HEADS = 4
DIFF_HEAD_DIM = 128
ROT_DIM = DIFF_HEAD_DIM // 4
LN_ROWS = 16
CONV_HALO = 32
VMEM_LIMIT = 56 * 1024 * 1024
MLP_VMEM_LIMIT = 60 * 1024 * 1024
LOG2E = 1.4426950408889634


def _params(*sem):
    return pltpu.CompilerParams(dimension_semantics=sem, vmem_limit_bytes=VMEM_LIMIT)


def _rms_scale(t):
    return lax.rsqrt(jnp.mean(t * t, axis=-1, keepdims=True) + EPS)


def _rmsnorm_kernel(x_ref, g_ref, o_ref):
    x = x_ref[...]
    o_ref[...] = ((x * _rms_scale(x)) * g_ref[...]).astype(o_ref.dtype)


def rmsnorm_bf16(x, g, tm):
    m, d = x.shape
    return pl.pallas_call(
        _rmsnorm_kernel,
        grid=(m // tm,),
        in_specs=[pl.BlockSpec((tm, d), lambda i: (i, 0)), pl.BlockSpec((1, d), lambda i: (0, 0))],
        out_specs=pl.BlockSpec((tm, d), lambda i: (i, 0)),
        out_shape=jax.ShapeDtypeStruct((m, d), BF16),
        compiler_params=_params("parallel"),
        name="rmsnorm",
    )(x, g.reshape(1, d))


def _rope_table_kernel(pos_ref, freq_ref, c_ref, sa_ref, sb_ref):
    ang = pos_ref[...] * freq_ref[...]
    lane = lax.broadcasted_iota(jnp.int32, ang.shape, 1)
    sin = jnp.sin(ang)
    c_ref[...] = jnp.where(lane < ROT_DIM, jnp.cos(ang), 1.0)
    sa_ref[...] = jnp.where(lane < ROT_DIM // 2, -sin, 0.0)
    sb_ref[...] = jnp.where((lane >= ROT_DIM // 2) & (lane < ROT_DIM), sin, 0.0)


def rope_tables(positions, tm):
    t = positions.size
    pos = jnp.broadcast_to(positions.reshape(t, 1).astype(F32), (t, LANES))
    inv_freq = jnp.power(jnp.float32(ROPE_THETA), -jnp.arange(0, ROT_DIM, 2, dtype=F32) / ROT_DIM)
    freq = jnp.concatenate([inv_freq, inv_freq, jnp.zeros((LANES - ROT_DIM,), F32)]).reshape(1, LANES)
    spec = pl.BlockSpec((tm, LANES), lambda i: (i, 0))
    shp = jax.ShapeDtypeStruct((t, LANES), F32)
    return pl.pallas_call(
        _rope_table_kernel,
        grid=(t // tm,),
        in_specs=[spec, pl.BlockSpec((1, LANES), lambda i: (0, 0))],
        out_specs=[spec, spec, spec],
        out_shape=[shp, shp, shp],
        compiler_params=_params("parallel"),
        name="rope_tables",
    )(pos, freq)


def _mm_kernel(a_ref, w_ref, o_ref, *, scale):
    acc = jnp.dot(a_ref[...], w_ref[...], preferred_element_type=F32)
    if scale != 1.0:
        acc = acc * scale
    o_ref[...] = acc.astype(o_ref.dtype)


def _col_window(w, tn, col0, ncols):
    n = w.shape[1] - col0 if ncols is None else ncols
    tn = _tile(math.gcd(n, col0) if col0 else n, tn)
    return n, tn, col0 // tn


def matmul(a, w, tm, tn, scale=1.0, out_dtype=BF16, col0=0, ncols=None):
    m, k = a.shape
    n, tn, j0 = _col_window(w, tn, col0, ncols)
    return pl.pallas_call(
        functools.partial(_mm_kernel, scale=scale),
        grid=(m // tm, n // tn),
        in_specs=[pl.BlockSpec((tm, k), lambda i, j: (i, 0)), pl.BlockSpec((k, tn), lambda i, j: (0, j + j0))],
        out_specs=pl.BlockSpec((tm, tn), lambda i, j: (i, j)),
        out_shape=jax.ShapeDtypeStruct((m, n), out_dtype),
        compiler_params=_params("parallel", "arbitrary"),
        name="matmul",
    )(a, w)


def _mm_nt_kernel(wt_ref, a_ref, o_ref):
    acc = lax.dot_general(wt_ref[...], a_ref[...], (((1,), (1,)), ((), ())), preferred_element_type=F32)
    o_ref[...] = acc.astype(o_ref.dtype)


def matmul_transposed_out(a, wt, tm, tn):
    m, k = a.shape
    n = wt.shape[0]
    tn = _tile(n, tn)
    return pl.pallas_call(
        _mm_nt_kernel,
        grid=(m // tm, n // tn),
        in_specs=[pl.BlockSpec((tn, k), lambda i, j: (j, 0)), pl.BlockSpec((tm, k), lambda i, j: (i, 0))],
        out_specs=pl.BlockSpec((tn, tm), lambda i, j: (j, i)),
        out_shape=jax.ShapeDtypeStruct((n, m), BF16),
        compiler_params=_params("parallel", "arbitrary"),
        name="matmul_transposed_out",
    )(wt, a)


def _mm_rope_kernel(a_ref, w_ref, c_ref, sa_ref, sb_ref, o_ref, *, scale):
    acc = jnp.dot(a_ref[...], w_ref[...], preferred_element_type=F32)
    c = c_ref[...] * scale
    sa = sa_ref[...] * scale
    sb = sb_ref[...] * scale
    for g in range(acc.shape[1] // LANES):
        t = acc[:, g * LANES:(g + 1) * LANES]
        r = t * c + pltpu.roll(t, LANES - ROT_DIM // 2, 1) * sa + pltpu.roll(t, ROT_DIM // 2, 1) * sb
        o_ref[:, g * LANES:(g + 1) * LANES] = r.astype(o_ref.dtype)


def matmul_rope(a, w, tables, tm, tn, scale=1.0, ncols=None):
    m, k = a.shape
    n, tn, _ = _col_window(w, tn, 0, ncols)
    tab = pl.BlockSpec((tm, LANES), lambda i, j: (i, 0))
    return pl.pallas_call(
        functools.partial(_mm_rope_kernel, scale=scale),
        grid=(m // tm, n // tn),
        in_specs=[pl.BlockSpec((tm, k), lambda i, j: (i, 0)), pl.BlockSpec((k, tn), lambda i, j: (0, j)),
                  tab, tab, tab],
        out_specs=pl.BlockSpec((tm, tn), lambda i, j: (i, j)),
        out_shape=jax.ShapeDtypeStruct((m, n), BF16),
        compiler_params=_params("parallel", "arbitrary"),
        name="matmul_rope",
    )(a, w, *tables)


def _mm_glu_kernel(a_ref, wu_ref, wg_ref, o_ref):
    a = a_ref[...]
    u = jnp.dot(a, wu_ref[...], preferred_element_type=F32)
    g = jnp.dot(a, wg_ref[...], preferred_element_type=F32)
    o_ref[...] = u * jax.nn.sigmoid(g)


def matmul_glu(a, w, n, tm, tn):
    m, k = a.shape
    tn = _tile(n, tn)
    nj = n // tn
    return pl.pallas_call(
        _mm_glu_kernel,
        grid=(m // tm, nj),
        in_specs=[pl.BlockSpec((tm, k), lambda i, j: (i, 0)), pl.BlockSpec((k, tn), lambda i, j: (0, j)),
                  pl.BlockSpec((k, tn), lambda i, j: (0, j + nj))],
        out_specs=pl.BlockSpec((tm, tn), lambda i, j: (i, j)),
        out_shape=jax.ShapeDtypeStruct((m, n), F32),
        compiler_params=_params("parallel", "arbitrary"),
        name="matmul_glu",
    )(a, w, w)


def _mm2_kernel(a1_ref, a2_ref, w1_ref, w2_ref, o_ref):
    acc = (jnp.dot(a1_ref[...], w1_ref[...], preferred_element_type=F32)
           + jnp.dot(a2_ref[...], w2_ref[...], preferred_element_type=F32))
    o_ref[...] = acc.astype(o_ref.dtype)


def matmul_concat(a1, a2, w, tm, tn):
    m, k1 = a1.shape
    k2 = a2.shape[1]
    n = w.shape[1]
    tn = _tile(n, tn)
    assert k1 % k2 == 0 and w.shape[0] == k1 + k2
    return pl.pallas_call(
        _mm2_kernel,
        grid=(m // tm, n // tn),
        in_specs=[pl.BlockSpec((tm, k1), lambda i, j: (i, 0)), pl.BlockSpec((tm, k2), lambda i, j: (i, 0)),
                  pl.BlockSpec((k1, tn), lambda i, j: (0, j)), pl.BlockSpec((k2, tn), lambda i, j: (k1 // k2, j))],
        out_specs=pl.BlockSpec((tm, tn), lambda i, j: (i, j)),
        out_shape=jax.ShapeDtypeStruct((m, n), BF16),
        compiler_params=_params("parallel", "arbitrary"),
        name="matmul_concat",
    )(a1, a2, w, w)


def _plain_cast(stacked, layer, ncols=None):
    return stacked[layer, :, :ncols].astype(BF16)


def _side_cast_specs(stacked, layer, steps, step_id, ncols=None):
    rows_total, cols = stacked.shape[1:]
    cols = ncols or cols
    if rows_total % steps or stacked.shape[2] % cols:
        return None
    rows = rows_total // steps
    hold = max(1, BF16_SUBLANES // rows)
    if (rows * hold) % BF16_SUBLANES or rows_total % (rows * hold):
        return None
    in_spec = pl.BlockSpec((None, rows * hold, cols), lambda *g: (layer, step_id(*g) // hold, 0))
    out_spec = pl.BlockSpec((rows * hold, cols), lambda *g: (step_id(*g) // hold, 0))
    return in_spec, out_spec, jax.ShapeDtypeStruct((rows_total, cols), BF16)


def _side_casts(arrays, steps, step_id):
    specs = [_side_cast_specs(e[0], e[1], steps, step_id, *e[2:]) for e in arrays]
    if any(s is None for s in specs):
        return None
    return ([e[0] for e in arrays], [s[0] for s in specs], [s[1] for s in specs], [s[2] for s in specs])


def _run_side_casts(in_refs, out_refs):
    for i_ref, o_ref in zip(in_refs, out_refs):
        o_ref[...] = i_ref[...].astype(o_ref.dtype)


def _mlp_kernel(h_ref, wu_ref, wd_ref, *rest, n_side):
    side_in, o_ref, side_out, acc_ref = rest[:n_side], rest[n_side], rest[n_side + 1:-1], rest[-1]

    @pl.when(pl.program_id(1) == 0)
    def _():
        acc_ref[...] = jnp.zeros_like(acc_ref)

    a = jnp.dot(h_ref[...], wu_ref[...], preferred_element_type=F32)
    a = jnp.maximum(a, 0.0)
    a = (a * a).astype(BF16)
    acc_ref[...] += jnp.dot(a, wd_ref[...], preferred_element_type=F32)
    _run_side_casts(side_in, side_out)

    @pl.when(pl.program_id(1) == pl.num_programs(1) - 1)
    def _():
        o_ref[...] = acc_ref[...].astype(o_ref.dtype)


def mlp(h, w_up, w_down, tm, tf, cast_arrays=()):
    m, d = h.shape
    f = w_up.shape[1]
    nj = f // tf
    side = _side_casts(cast_arrays, (m // tm) * nj, lambda i, j: i * nj + j) if cast_arrays else None
    ops, s_in, s_out, s_shape = side if side else ([], [], [], [])
    outs = pl.pallas_call(
        functools.partial(_mlp_kernel, n_side=len(ops)),
        grid=(m // tm, nj),
        in_specs=[pl.BlockSpec((tm, d), lambda i, j: (i, 0), pipeline_mode=pl.Buffered(1 if ops else 2)),
                  pl.BlockSpec((d, tf), lambda i, j: (0, j)), pl.BlockSpec((tf, d), lambda i, j: (j, 0))] + s_in,
        out_specs=[pl.BlockSpec((tm, d), lambda i, j: (i, 0), pipeline_mode=pl.Buffered(1))] + s_out,
        out_shape=[jax.ShapeDtypeStruct((m, d), BF16)] + s_shape,
        scratch_shapes=[pltpu.VMEM((tm, d), F32)],
        compiler_params=pltpu.CompilerParams(dimension_semantics=("arbitrary", "arbitrary"),
                                             vmem_limit_bytes=MLP_VMEM_LIMIT),
        name="mlp",
    )(h, w_up, w_down, *ops)
    casts = list(outs[1:]) if side else [_plain_cast(*entry) for entry in cast_arrays]
    return outs[0], casts


def _norm_residual_kernel(x_ref, y_ref, gp_ref, *rest, n_next):
    g_refs, x_out, h_outs = rest[:n_next], rest[n_next], rest[n_next + 1:]
    y = y_ref[...].astype(F32)
    x = x_ref[...] + (y * _rms_scale(y)) * gp_ref[...]
    x_out[...] = x
    if n_next:
        xn = x * _rms_scale(x)
        for g_ref, h_out in zip(g_refs, h_outs):
            h_out[...] = (xn * g_ref[...]).astype(h_out.dtype)


def norm_residual(x, y, g_post, g_next, tm):
    m, d = x.shape
    row = pl.BlockSpec((tm, d), lambda i: (i, 0))
    vec = pl.BlockSpec((1, d), lambda i: (0, 0))
    n_next = len(g_next)
    outs = pl.pallas_call(
        functools.partial(_norm_residual_kernel, n_next=n_next),
        grid=(m // tm,),
        in_specs=[row, row, vec] + [vec] * n_next,
        out_specs=[row] + [row] * n_next,
        out_shape=[jax.ShapeDtypeStruct((m, d), F32)] + [jax.ShapeDtypeStruct((m, d), BF16)] * n_next,
        compiler_params=_params("parallel"),
        name="norm_residual",
    )(x, y, g_post.reshape(1, d), *[g.reshape(1, d) for g in g_next])
    return outs[0], list(outs[1:])


def _conv_ln_kernel(halo_ref, cur_ref, w_ref, b_ref, g_ref, beta_ref, *rest, width, rows, cw, n_side):
    side_in, o_ref, side_out = rest[:n_side], rest[n_side], rest[n_side + 1:2 * n_side + 1]
    buf_ref, acc_ref = rest[2 * n_side + 1:]
    _run_side_casts(side_in, side_out)
    ts, ch = cur_ref.shape
    first = pl.program_id(1) == 0
    buf_ref[0:CONV_HALO, :] = jnp.where(first, 0.0, halo_ref[...])
    buf_ref[CONV_HALO:, :] = cur_ref[...]
    lead = CONV_HALO - (width - 1)
    win = rows + CONV_HALO

    def chunk(ci, carry):
        cols = pl.ds(pl.multiple_of(ci * cw, cw), cw)
        for r0 in range(0, ts, rows):
            x = buf_ref[r0:r0 + win, cols]
            acc = b_ref[:, cols] + buf_ref[CONV_HALO + r0:CONV_HALO + r0 + rows, cols] * w_ref[width - 1:width, cols]
            for b in range(8):
                xb = x if b == 0 else pltpu.roll(x, win - b, 0)
                for a in range(CONV_HALO // 8):
                    k = 8 * a + b - lead
                    if 0 <= k < width - 1:
                        acc = acc + xb[8 * a:8 * a + rows] * w_ref[k:k + 1, cols]
            acc_ref[r0:r0 + rows, cols] = acc
        return carry

    lax.fori_loop(0, ch // cw, chunk, 0)

    def norm_rows(ri, carry):
        rs = pl.ds(pl.multiple_of(ri * LN_ROWS, LN_ROWS), LN_ROWS)
        c = acc_ref[rs, :]
        mu = jnp.mean(c, axis=-1, keepdims=True)
        cc = c - mu
        var = jnp.mean(cc * cc, axis=-1, keepdims=True)
        y = (cc * lax.rsqrt(var + EPS)) * g_ref[...] + beta_ref[...]
        o_ref[rs, :] = (y * jax.nn.sigmoid(y)).astype(o_ref.dtype)
        return carry

    lax.fori_loop(0, ts // LN_ROWS, norm_rows, 0, unroll=4)


def conv_ln_silu(c0, w, b, g, beta, batch, ts, rows=64, cw=128, cast_arrays=()):
    t, ch = c0.shape
    seq = t // batch
    width = w.shape[0]
    nt = seq // ts
    hb = ts // CONV_HALO
    vec = pl.BlockSpec((1, ch), lambda bi, i: (0, 0))
    side = _side_casts(cast_arrays, batch * nt, lambda bi, i: bi * nt + i) if cast_arrays else None
    ops, s_in, s_out, s_shape = side if side else ([], [], [], [])
    outs = pl.pallas_call(
        functools.partial(_conv_ln_kernel, width=width, rows=rows, cw=cw, n_side=len(ops)),
        grid=(batch, nt),
        in_specs=[pl.BlockSpec((CONV_HALO, ch), lambda bi, i: (jnp.maximum((bi * nt + i) * hb - 1, 0), 0)),
                  pl.BlockSpec((ts, ch), lambda bi, i: (bi * nt + i, 0)),
                  pl.BlockSpec((width, ch), lambda bi, i: (0, 0)), vec, vec, vec] + s_in,
        out_specs=[pl.BlockSpec((ts, ch), lambda bi, i: (bi * nt + i, 0))] + s_out,
        out_shape=[jax.ShapeDtypeStruct((t, ch), BF16)] + s_shape,
        scratch_shapes=[pltpu.VMEM((ts + CONV_HALO, ch), F32), pltpu.VMEM((ts, ch), F32)],
        compiler_params=_params("arbitrary", "arbitrary"),
        name="conv_ln_silu",
    )(c0, c0, w, b.reshape(1, ch), g.reshape(1, ch), beta.reshape(1, ch), *ops)
    casts = list(outs[1:]) if side else [_plain_cast(*entry) for entry in cast_arrays]
    return outs[0], casts


def _mem_attn_kernel(q_ref, kv_ref, o_ref):
    width = q_ref.shape[1]
    hd = width // MEM_HEADS
    for h in range(MEM_HEADS):
        q = q_ref[:, h * hd:(h + 1) * hd]
        mk = kv_ref[:, h * hd:(h + 1) * hd]
        mv = kv_ref[:, width + h * hd:width + (h + 1) * hd]
        s = lax.dot_general(q, mk, (((1,), (1,)), ((), ())), preferred_element_type=F32)
        p = jnp.exp(s - jnp.max(s, axis=-1, keepdims=True))
        l = jnp.sum(p, axis=-1, keepdims=True)
        o = jnp.dot(p.astype(BF16), mv, preferred_element_type=F32)
        o_ref[:, h * hd:(h + 1) * hd] = (o / l).astype(o_ref.dtype)


def mem_attention(qm, mkv, batch, tm):
    t, width = qm.shape
    nt = t // batch // tm
    mt = mkv.shape[0] // batch
    return pl.pallas_call(
        _mem_attn_kernel,
        grid=(batch, nt),
        in_specs=[pl.BlockSpec((tm, width), lambda b, i: (b * nt + i, 0)),
                  pl.BlockSpec((mt, 2 * width), lambda b, i: (b, 0))],
        out_specs=pl.BlockSpec((tm, width), lambda b, i: (b * nt + i, 0)),
        out_shape=jax.ShapeDtypeStruct((t, width), BF16),
        compiler_params=_params("parallel", "arbitrary"),
        name="mem_attention",
    )(qm, mkv)


def _diff_attn_kernel(lam_ref, g_ref, q_ref, k_ref, vt_ref, o_ref, *scratch, lambda_init, tq):
    per_step = q_ref.shape[0] // tq
    for sub in range(per_step):
        rows = pl.ds(sub * tq, tq)
        _diff_attn_block(pl.program_id(2) * per_step + sub, lam_ref, g_ref, q_ref.at[rows], k_ref, vt_ref,
                         o_ref.at[rows], *scratch, lambda_init=lambda_init)


def _diff_attn_block(qi, lam_ref, g_ref, q_ref, k_ref, vt_ref, o_ref, acc1_ref, acc2_ref, sa_ref, sb_ref, st_ref,
                     *, lambda_init):
    tq = q_ref.shape[0]
    d = DIFF_HEAD_DIM
    q = q_ref[...]
    qs = (q[:, :d], q[:, d:])
    accs = (acc1_ref, acc2_ref)
    for acc in accs:
        acc[...] = jnp.zeros_like(acc)
    st_ref[...] = jnp.where(lax.broadcasted_iota(jnp.int32, st_ref.shape, 0) % 2 == 0, -jnp.inf, 0.0)
    key = lax.broadcasted_iota(jnp.int32, (tq, tq), 0)
    qry = lax.broadcasted_iota(jnp.int32, (tq, tq), 1)
    causal = key <= qry
    nt = (((1,), (1,)), ((), ()))

    def scores(j, s_ref):
        k = k_ref[pl.ds(pl.multiple_of(j * tq, tq), tq), :]
        for c in range(2):
            s_ref[c] = lax.dot_general(k[:, c * d:(c + 1) * d], qs[c], nt, preferred_element_type=F32)

    def update(j, s_ref, masked):
        vt = vt_ref[:, pl.ds(pl.multiple_of(j * tq, tq), tq)]
        for c in range(2):
            m, l = st_ref[2 * c:2 * c + 1, :], st_ref[2 * c + 1:2 * c + 2, :]
            s = s_ref[c]
            if masked:
                s = jnp.where(causal, s, -jnp.inf)
            mn = jnp.maximum(m, jnp.max(s, axis=0, keepdims=True))
            p = jnp.exp2(s - mn)
            alpha = jnp.exp2(m - mn)
            st_ref[2 * c:2 * c + 1, :] = mn
            st_ref[2 * c + 1:2 * c + 2, :] = alpha * l + jnp.sum(p, axis=0, keepdims=True)
            accs[c][...] = accs[c][...] * alpha + jnp.dot(vt, p.astype(BF16), preferred_element_type=F32)

    scores(0, sa_ref)

    def pairs(j, count):
        for _ in range(count):
            scores(j + 1, sb_ref)
            update(j, sa_ref, False)
            scores(j + 2, sa_ref)
            update(j + 1, sb_ref, False)
            j = j + 2

    def quad(i, carry):
        pairs(4 * i, 2)
        return carry

    def pair(i, carry):
        pairs(4 * (qi // 4) + 2 * i, 1)
        return carry

    lax.fori_loop(0, qi // 4, quad, 0)
    lax.fori_loop(0, (qi % 4) // 2, pair, 0)

    @pl.when(qi % 2 == 1)
    def _():
        scores(qi, sb_ref)
        update(qi - 1, sa_ref, False)
        update(qi, sb_ref, True)

    @pl.when(qi % 2 == 0)
    def _():
        update(qi, sa_ref, True)

    l1, l2 = st_ref[1:2, :], st_ref[3:4, :]
    lm = lam_ref[...]
    lam = (jnp.exp(jnp.sum(lm[0:1] * lm[1:2], axis=-1, keepdims=True))
           - jnp.exp(jnp.sum(lm[2:3] * lm[3:4], axis=-1, keepdims=True)) + lambda_init)
    o = acc1_ref[...] / l1 - lam * (acc2_ref[...] / l2)
    inv = lax.rsqrt(jnp.mean(o * o, axis=0, keepdims=True) + EPS)
    o = (o * inv) * g_ref[...] * (1.0 - lambda_init)
    o_ref[...] = o.T.astype(o_ref.dtype)


def diff_attention(q, k, vt, lam_params, subln, batch, tq, lambda_init):
    t, width = q.shape
    seq = t // batch
    hw = 2 * DIFF_HEAD_DIM
    heads = width // hw
    per_step = 2 if (seq // tq) % 2 == 0 else 1
    nq = seq // (tq * per_step)
    return pl.pallas_call(
        functools.partial(_diff_attn_kernel, lambda_init=lambda_init, tq=tq),
        grid=(batch, heads, nq),
        in_specs=[pl.BlockSpec((4, DIFF_HEAD_DIM), lambda b, h, i: (0, 0)),
                  pl.BlockSpec((hw, 1), lambda b, h, i: (0, 0)),
                  pl.BlockSpec((tq * per_step, hw), lambda b, h, i: (b * nq + i, h)),
                  pl.BlockSpec((seq, hw), lambda b, h, i: (b, h)),
                  pl.BlockSpec((hw, seq), lambda b, h, i: (h, b))],
        out_specs=pl.BlockSpec((tq * per_step, hw), lambda b, h, i: (b * nq + i, h)),
        out_shape=jax.ShapeDtypeStruct((t, width), BF16),
        scratch_shapes=[pltpu.VMEM((hw, tq), F32), pltpu.VMEM((hw, tq), F32),
                        pltpu.VMEM((2, tq, tq), F32), pltpu.VMEM((2, tq, tq), F32), pltpu.VMEM((8, tq), F32)],
        compiler_params=_params("parallel", "parallel", "arbitrary"),
        name="diff_attention",
    )(lam_params, subln.reshape(hw, 1), q, k, vt)


def _tile(n, want):
    t = min(n, want)
    while n % t:
        t //= 2
    return t


def kernel(x, mem, positions, a_norm_pre, a_norm_post, a_mem_norm, a_w_in, a_conv_w, a_conv_b, a_ln_g, a_ln_b,
           a_w_mem_kv, a_w_out, kv_norm, w_kv, b_norm_pre, b_norm_post, b_mem_norm, b_w_in, b_lambda, b_subln,
           b_w_mem_kv, b_w_out, mlp_norm_pre, mlp_norm_post, w_up, w_down):
    batch, seq, d = x.shape
    t = batch * seq
    n_a = a_norm_pre.shape[0]
    depth = n_a + b_norm_pre.shape[0]
    conv_ch = a_conv_w.shape[2]
    mem_width = a_w_in.shape[2] - 2 * conv_ch
    qk_width = b_w_in.shape[2] - mem_width
    mem_scale = float((mem_width // MEM_HEADS) ** -0.5)

    tm = _tile(t, 1024)
    tn = 1024
    tn_glu = 512
    te = _tile(t, 256)
    mem2 = mem.reshape(-1, d)
    tmem = _tile(mem2.shape[0], 512)

    xr = x.reshape(t, d)
    tables = rope_tables(positions, _tile(t, 1024))
    h = rmsnorm_bf16(xr, a_norm_pre[0] if n_a else b_norm_pre[0], te)
    k_sh = v_sh = None
    mlp_w = None
    pre = {}

    def bf16_weight(name, stacked, idx):
        return pre[name] if name in pre else stacked[idx].astype(BF16)

    for layer in range(depth):
        last = layer == depth - 1
        if layer < n_a:
            i = layer
            w_in = bf16_weight("w_in", a_w_in, i)
            c0 = matmul_glu(h, w_in, conv_ch, tm, tn_glu)
            qm = matmul(h, w_in, tm, tn, scale=mem_scale, col0=2 * conv_ch)
            cast = () if mlp_w else ((w_up, layer), (w_down, layer))
            main, casts = conv_ln_silu(c0, a_conv_w[i], a_conv_b[i], a_ln_g[i], a_ln_b[i], batch, _tile(seq, 256),
                                       cast_arrays=cast)
            mlp_w = mlp_w or casts
            mem_n = rmsnorm_bf16(mem2, a_mem_norm[i], _tile(mem2.shape[0], 256))
            mkv = matmul(mem_n, bf16_weight("w_mem_kv", a_w_mem_kv, i), tmem, tn)
            w_out = bf16_weight("w_out", a_w_out, i)
            g_post = a_norm_post[i]
        else:
            j = layer - n_a
            lambda_init = 0.8 - 0.6 * math.exp(-0.3 * layer)
            w_in = bf16_weight("w_in", b_w_in, j)
            q = matmul_rope(h, w_in, tables, tm, tn, scale=DIFF_HEAD_DIM ** -0.5 * LOG2E, ncols=qk_width)
            qm = matmul(h, w_in, tm, tn, scale=mem_scale, col0=qk_width)
            main = diff_attention(q, k_sh, v_sh, b_lambda[j], b_subln[j], batch, _tile(seq, 512), lambda_init)
            mem_n = rmsnorm_bf16(mem2, b_mem_norm[j], _tile(mem2.shape[0], 256))
            mkv = matmul(mem_n, bf16_weight("w_mem_kv", b_w_mem_kv, j), tmem, tn)
            w_out = bf16_weight("w_out", b_w_out, j)
            g_post = b_norm_post[j]

        m_out = mem_attention(qm, mkv, batch, _tile(seq, 512))
        y = matmul_concat(main, m_out, w_out, tm, tn)
        xr, (h,) = norm_residual(xr, y, g_post, [mlp_norm_pre[layer]], te)

        mlp_w = mlp_w or [w_up[layer].astype(BF16), w_down[layer].astype(BF16)]
        nxt, names = [], []
        if not last:
            nl = layer + 1
            proj = (a_w_in, a_w_mem_kv, a_w_out, nl) if nl < n_a else (b_w_in, b_w_mem_kv, b_w_out, nl - n_a)
            nxt = [(w_up, nl), (w_down, nl)] + [(w, proj[3]) for w in proj[:3]]
            names = ["w_in", "w_mem_kv", "w_out"]
        if layer == n_a - 1:
            nxt.append((w_kv[None], 0, qk_width))
            names.append("w_k")
        y, casts = mlp(h, mlp_w[0], mlp_w[1], tm, 512, cast_arrays=nxt)
        mlp_w = casts[:2] if not last else None
        pre = dict(zip(names, casts[2:] if not last else casts))
        g_next = []
        if layer == n_a - 1:
            g_next.append(kv_norm)
        if not last:
            g_next.append(a_norm_pre[layer + 1] if layer + 1 < n_a else b_norm_pre[layer + 1 - n_a])
        xr, hs = norm_residual(xr, y, mlp_norm_post[layer], g_next, te)
        if layer == n_a - 1:
            k_sh = matmul_rope(hs[0], pre.pop("w_k"), tables, tm, tn)
            v_sh = matmul_transposed_out(hs[0], w_kv[:, qk_width:].T.astype(BF16), tm, 512)
        if not last:
            h = hs[-1]

    return xr.reshape(batch, seq, d)
```
